```python
import math
import jax, jax.numpy as jnp
from jax import lax
import numpy as np

D_MODEL = 1024
BATCH = 16
SEQ = 2048
DEPTH = 2

CHUNK = 64
MEM_LEN = 256
EPS = 1e-6

FOX_HEAD_DIM = 64
FOX_HEADS = D_MODEL // 128
FOX_WIDTH = FOX_HEADS * FOX_HEAD_DIM
FOX_Q_BLOCK = 128

MLSTM_HEADS = 4
MLSTM_HEAD_DIM = D_MODEL // 8
MLSTM_WIDTH = MLSTM_HEADS * MLSTM_HEAD_DIM
MLSTM_CHUNK = CHUNK
CONV_WIDTH = 4

GMLP_GROUPS = 4
GMLP_GROUP_DIM = D_MODEL // 8
GMLP_WIDTH = GMLP_GROUPS * GMLP_GROUP_DIM
GMLP_SPAN = 128

N_BRANCH = 3
BRANCH_WIDTH = D_MODEL // 2

XATTN_HEADS = 4
XATTN_HEAD_DIM = D_MODEL // XATTN_HEADS

D_FF = 4 * D_MODEL

NORM_MIX_PRE, NORM_MIX_POST, NORM_X_PRE, NORM_X_POST, NORM_MEM, NORM_FF_PRE, NORM_FF_POST = 0, 1, 2, 3, 4, 5, 6
N_NORMS = 7

IN_SPLITS = (
    ("fox_q", FOX_WIDTH), ("fox_k", FOX_WIDTH), ("fox_v", FOX_WIDTH), ("fox_f", FOX_HEADS),
    ("ml_q", MLSTM_WIDTH), ("ml_k", MLSTM_WIDTH), ("ml_v", MLSTM_WIDTH),
    ("ml_i", MLSTM_HEADS), ("ml_f", MLSTM_HEADS), ("ml_o", MLSTM_WIDTH),
    ("g_u", GMLP_WIDTH), ("g_v", GMLP_WIDTH),
    ("gate", N_BRANCH * D_MODEL),
)
D_IN = sum(size for _, size in IN_SPLITS)

kernel_name = "hybrid_fox_mlstm_gmlp_streaming_encoder"


def rms_norm(x, g):
    x32 = x.astype(jnp.float32)
    y = x32 * lax.rsqrt(jnp.mean(x32 * x32, axis=-1, keepdims=True) + EPS)
    return y.astype(x.dtype) * g


def layer_norm(x, g):
    x32 = x.astype(jnp.float32)
    mu = jnp.mean(x32, axis=-1, keepdims=True)
    xc = x32 - mu
    y = xc * lax.rsqrt(jnp.mean(xc * xc, axis=-1, keepdims=True) + EPS)
    return y.astype(x.dtype) * g


def split_columns(z):
    parts = []
    start = 0
    for _, size in IN_SPLITS:
        parts.append(z[..., start:start + size])
        start += size
    return parts


def causal_conv(x, w):
    K = w.shape[0]
    S = x.shape[1]
    xp = jnp.pad(x, ((0, 0), (K - 1, 0), (0, 0)))
    y = w[0] * xp[:, 0:S]
    for j in range(1, K):
        y = y + w[j] * xp[:, j:j + S]
    return y


def fox_attention(q, k, v, f_pre):
    B, S, H, dh = q.shape
    F = jnp.cumsum(jax.nn.log_sigmoid(f_pre.astype(jnp.float32)), axis=1)
    F = F.transpose(0, 2, 1)
    q = q.transpose(0, 2, 1, 3) * (dh ** -0.5)
    k = k.transpose(0, 2, 1, 3)
    v = v.transpose(0, 2, 1, 3)
    outs = []
    for blk in range(S // FOX_Q_BLOCK):
        q0 = blk * FOX_Q_BLOCK
        q1 = q0 + FOX_Q_BLOCK
        s = jnp.einsum('bhqd,bhkd->bhqk', q[:, :, q0:q1], k[:, :, :q1]).astype(jnp.float32)
        s = s + F[:, :, q0:q1, None] - F[:, :, None, :q1]
        causal = jnp.arange(q0, q1)[:, None] >= jnp.arange(q1)[None, :]
        s = jnp.where(causal, s, -jnp.inf)
        p = jax.nn.softmax(s, axis=-1).astype(v.dtype)
        outs.append(jnp.einsum('bhqk,bhkd->bqhd', p, v[:, :, :q1]))
    return jnp.concatenate(outs, axis=1).reshape(B, S, H * dh)


def mlstm(q, k, v, i_pre, f_pre):
    B, S, H, d = q.shape
    L = MLSTM_CHUNK
    NC = S // L
    out_dtype = q.dtype

    def to_chunks(a):
        a = a.astype(jnp.float32).reshape((B, NC, L, H) + a.shape[3:])
        return jnp.moveaxis(a, (1, 3), (0, 2))

    qc = to_chunks(q)
    kc = to_chunks(k) * (d ** -0.5)
    vc = to_chunks(v)
    ic = to_chunks(i_pre)
    lfc = to_chunks(jax.nn.log_sigmoid(f_pre.astype(jnp.float32)))
    tril = jnp.tril(jnp.ones((L, L), dtype=bool))

    def step(carry, xs):
        C, n, m_prev = carry
        qb, kb, vb, ib, lfb = xs
        b = jnp.cumsum(lfb, axis=-1)
        D = jnp.where(tril, b[..., :, None] - b[..., None, :] + ib[..., None, :], -jnp.inf)
        inter = b + m_prev[..., None]
        m = jnp.maximum(inter, jnp.max(D, axis=-1))
        w_inter = jnp.exp(inter - m)
        P = jnp.einsum('bhtd,bhsd->bhts', qb, kb) * jnp.exp(D - m[..., None])
        num = w_inter[..., None] * jnp.einsum('bhtd,bhde->bhte', qb, C) + jnp.einsum('bhts,bhse->bhte', P, vb)
        den = w_inter * jnp.einsum('bhtd,bhd->bht', qb, n) + jnp.sum(P, axis=-1)
        h = num / jnp.maximum(jnp.abs(den), jnp.exp(-m))[..., None]
        m_new = m[..., -1]
        decay = jnp.exp(b[..., -1] + m_prev - m_new)
        w_s = jnp.exp(b[..., -1:] - b + ib - m_new[..., None])
        C_new = decay[..., None, None] * C + jnp.einsum('bhs,bhsd,bhse->bhde', w_s, kb, vb)
        n_new = decay[..., None] * n + jnp.einsum('bhs,bhsd->bhd', w_s, kb)
        return (C_new, n_new, m_new), h

    init = (jnp.zeros((B, H, d, d), jnp.float32),
            jnp.zeros((B, H, d), jnp.float32),
            jnp.zeros((B, H), jnp.float32))
    _, h = lax.scan(step, init, (qc, kc, vc, ic, lfc))
    h = jnp.moveaxis(h, (0, 2), (1, 3)).reshape(B, S, H, d)
    return h.astype(out_dtype)


def gmlp_sgu(u, v, norm_g, ws, bs):
    B, S, _ = u.shape
    u = jax.nn.gelu(u)
    v = layer_norm(jax.nn.gelu(v), norm_g)
    NG = S // GMLP_SPAN
    vg = v.reshape(B, NG, GMLP_SPAN, GMLP_GROUPS, GMLP_GROUP_DIM)
    pos_chunk = jnp.arange(GMLP_SPAN) // CHUNK
    mask = pos_chunk[:, None] >= pos_chunk[None, :]
    ws = jnp.where(mask, ws, 0)
    mixed = jnp.einsum('gts,bnsgc->bntgc', ws, vg) + bs.T[:, :, None]
    return u * mixed.reshape(B, S, GMLP_WIDTH)


def hybrid_mixer(h, w_in, b_in, conv_w, mlstm_g, gmlp_g, gmlp_ws, gmlp_bs, w_branch, w_out):
    B, S, _ = h.shape
    z = h @ w_in + b_in
    (fq, fk, fv, ff, mq, mk, mv, mi, mf, mo, gu, gv, gate) = split_columns(z)

    shp_f = (B, S, FOX_HEADS, FOX_HEAD_DIM)
    y_fox = fox_attention(fq.reshape(shp_f), fk.reshape(shp_f), fv.reshape(shp_f), ff)

    qk = jax.nn.silu(causal_conv(jnp.concatenate([mq, mk], axis=-1), conv_w))
    mq, mk = qk[..., :MLSTM_WIDTH], qk[..., MLSTM_WIDTH:]
    shp_m = (B, S, MLSTM_HEADS, MLSTM_HEAD_DIM)
    hm = mlstm(mq.reshape(shp_m), mk.reshape(shp_m), mv.reshape(shp_m), mi, mf)
    hm = rms_norm(hm, mlstm_g.reshape(MLSTM_HEADS, MLSTM_HEAD_DIM)).reshape(B, S, MLSTM_WIDTH)
    y_ml = jax.nn.sigmoid(mo) * hm

    y_g = gmlp_sgu(gu, gv, gmlp_g, gmlp_ws, gmlp_bs)

    ys = jnp.stack([y_fox, y_ml, y_g], axis=2)
    branches = jnp.einsum('bsnc,ncd->bsnd', ys, w_branch)
    gates = jax.nn.sigmoid(gate.reshape(B, S, N_BRANCH, D_MODEL))
    merged = jnp.sum(gates * branches, axis=2)
    return merged @ w_out


def cross_attention(h, mem_n, w_q, w_kv, w_o):
    B, S, _ = h.shape
    M = mem_n.shape[1]
    q = (h @ w_q).reshape(B, S, XATTN_HEADS, XATTN_HEAD_DIM)
    kv = mem_n @ w_kv
    k = kv[..., :D_MODEL].reshape(B, M, XATTN_HEADS, XATTN_HEAD_DIM)
    v = kv[..., D_MODEL:].reshape(B, M, XATTN_HEADS, XATTN_HEAD_DIM)
    s = jnp.einsum('bqhd,bkhd->bhqk', q, k).astype(jnp.float32) * (XATTN_HEAD_DIM ** -0.5)
    p = jax.nn.softmax(s, axis=-1).astype(v.dtype)
    o = jnp.einsum('bhqk,bkhd->bqhd', p, v).reshape(B, S, D_MODEL)
    return o @ w_o


def setup_inputs(seed: int = 0) -> dict:
    key = jax.random.key(seed)
    ks = jax.random.split(key, 20)
    f32 = jnp.float32

    def normal(k, shape, scale):
        return jax.random.normal(k, shape, f32) * scale

    x = normal(ks[0], (BATCH, SEQ, D_MODEL), 1.0)
    mem = normal(ks[1], (BATCH, MEM_LEN, D_MODEL), 1.0)
    norms = 1.0 + normal(ks[2], (DEPTH, N_NORMS, D_MODEL), 0.05)
    w_in = normal(ks[3], (DEPTH, D_MODEL, D_IN), D_MODEL ** -0.5)
    seg_keys = jax.random.split(ks[4], len(IN_SPLITS))
    parts = []
    for (name, size), kk in zip(IN_SPLITS, seg_keys):
        if name == "fox_f":
            parts.append(jax.random.uniform(kk, (DEPTH, size), f32, 1.0, 4.0))
        elif name == "ml_f":
            parts.append(jax.random.uniform(kk, (DEPTH, size), f32, 3.0, 6.0))
        elif name == "ml_i":
            parts.append(normal(kk, (DEPTH, size), 0.1))
        else:
            parts.append(normal(kk, (DEPTH, size), 0.02))
    b_in = jnp.concatenate(parts, axis=-1)
    conv_w = normal(ks[5], (DEPTH, CONV_WIDTH, 2 * MLSTM_WIDTH), CONV_WIDTH ** -0.5)
    mlstm_norm = 1.0 + normal(ks[6], (DEPTH, MLSTM_WIDTH), 0.05)
    gmlp_norm = 1.0 + normal(ks[7], (DEPTH, GMLP_WIDTH), 0.05)
    gmlp_ws = normal(ks[8], (DEPTH, GMLP_GROUPS, GMLP_SPAN, GMLP_SPAN), GMLP_SPAN ** -0.5)
    gmlp_bs = 1.0 + normal(ks[9], (DEPTH, GMLP_GROUPS, GMLP_SPAN), 0.1)
    w_branch = normal(ks[10], (DEPTH, N_BRANCH, BRANCH_WIDTH, D_MODEL), BRANCH_WIDTH ** -0.5)
    w_out = normal(ks[11], (DEPTH, D_MODEL, D_MODEL), D_MODEL ** -0.5)
    w_xq = normal(ks[12], (DEPTH, D_MODEL, D_MODEL), D_MODEL ** -0.5)
    w_xkv = normal(ks[13], (DEPTH, D_MODEL, 2 * D_MODEL), D_MODEL ** -0.5)
    w_xo = normal(ks[14], (DEPTH, D_MODEL, D_MODEL), D_MODEL ** -0.5)
    w_ff1 = normal(ks[15], (DEPTH, D_MODEL, D_FF), D_MODEL ** -0.5)
    w_ff2 = normal(ks[16], (DEPTH, D_FF, D_MODEL), D_FF ** -0.5)
    return {"x": x, "mem": mem, "norms": norms, "w_in": w_in, "b_in": b_in,
            "conv_w": conv_w, "mlstm_norm": mlstm_norm, "gmlp_norm": gmlp_norm,
            "gmlp_ws": gmlp_ws, "gmlp_bs": gmlp_bs, "w_branch": w_branch, "w_out": w_out,
            "w_xq": w_xq, "w_xkv": w_xkv, "w_xo": w_xo, "w_ff1": w_ff1, "w_ff2": w_ff2}


def reference(x, mem, norms, w_in, b_in, conv_w, mlstm_norm, gmlp_norm, gmlp_ws, gmlp_bs,
              w_branch, w_out, w_xq, w_xkv, w_xo, w_ff1, w_ff2):
    for l in range(DEPTH):
        g = norms[l]
        h = rms_norm(x, g[NORM_MIX_PRE])
        y = hybrid_mixer(h, w_in[l], b_in[l], conv_w[l], mlstm_norm[l], gmlp_norm[l],
                         gmlp_ws[l], gmlp_bs[l], w_branch[l], w_out[l])
        x = x + rms_norm(y, g[NORM_MIX_POST])
        h = rms_norm(x, g[NORM_X_PRE])
        mem_n = rms_norm(mem, g[NORM_MEM])
        y = cross_attention(h, mem_n, w_xq[l], w_xkv[l], w_xo[l])
        x = x + rms_norm(y, g[NORM_X_POST])
        h = rms_norm(x, g[NORM_FF_PRE])
        y = jnp.square(jax.nn.relu(h @ w_ff1[l])) @ w_ff2[l]
        x = x + rms_norm(y, g[NORM_FF_POST])
    return x
```

```python
import functools

import jax
import jax.numpy as jnp
from jax import lax
from jax.experimental import pallas as pl
from jax.experimental.pallas import tpu as pltpu

F32 = jnp.float32
BF16 = jnp.bfloat16

EPS = 1e-6
D_MODEL = 1024
CHUNK = 64
FOX_HEADS = 8
FOX_HEAD_DIM = 64
FOX_WIDTH = FOX_HEADS * FOX_HEAD_DIM
ML_HEADS = 4
ML_HEAD_DIM = 128
ML_WIDTH = ML_HEADS * ML_HEAD_DIM
CONV_WIDTH = 4
G_GROUPS = 4
G_GROUP_DIM = 128
G_WIDTH = G_GROUPS * G_GROUP_DIM
G_SPAN = 128
N_BRANCH = 3
X_HEADS = 4
X_HEAD_DIM = D_MODEL // X_HEADS
D_FF = 4 * D_MODEL

LANES = 128
MIB = 1024 * 1024

_IN_SPLITS = (
    ("fox_q", FOX_WIDTH), ("fox_k", FOX_WIDTH), ("fox_v", FOX_WIDTH), ("fox_f", FOX_HEADS),
    ("ml_q", ML_WIDTH), ("ml_k", ML_WIDTH), ("ml_v", ML_WIDTH),
    ("ml_i", ML_HEADS), ("ml_f", ML_HEADS), ("ml_o", ML_WIDTH),
    ("g_u", G_WIDTH), ("g_v", G_WIDTH),
    ("gate", N_BRANCH * D_MODEL),
)
_IN_OFFSETS = {}
_off = 0
for _name, _size in _IN_SPLITS:
    _IN_OFFSETS[_name] = (_off, _size)
    _off += _size

_WIDE_ORDER = ("gate", "fox_q", "fox_k", "fox_v", "ml_q", "ml_k", "ml_v", "ml_o", "g_u", "g_v")
_WIDE_OFFSETS = {}
_off = 0
for _name in _WIDE_ORDER:
    _WIDE_OFFSETS[_name] = _off
    _off += _IN_OFFSETS[_name][1]
N_WIDE = _off
_NARROW_ORDER = ("fox_f", "ml_i", "ml_f")
LANE_FOX_F = 0
LANE_ML_I = FOX_HEADS
LANE_ML_F = FOX_HEADS + ML_HEADS
N_GATE_LANES = FOX_HEADS + 2 * ML_HEADS


def _rms(x, g):
    return x * lax.rsqrt(jnp.mean(x * x, axis=-1, keepdims=True) + EPS) * g


def _log_sigmoid(x):
    return jnp.minimum(x, 0.0) - jnp.log1p(jnp.exp(-jnp.abs(x)))


def _params(semantics, vmem_mib):
    return pltpu.CompilerParams(dimension_semantics=semantics, vmem_limit_bytes=vmem_mib * MIB)


def _in_proj_kernel(x_ref, g_ref, w_ref, b_ref, wn_ref, bn_ref, z_ref, zn_ref, h_ref):
    @pl.when(pl.program_id(1) == 0)
    def _():
        h = _rms(x_ref[...], g_ref[...]).astype(BF16)
        h_ref[...] = h
        zn_ref[...] = jnp.dot(h, wn_ref[...], preferred_element_type=F32) + bn_ref[...]

    z = jnp.dot(h_ref[...], w_ref[...], preferred_element_type=F32) + b_ref[...]
    z_ref[...] = z.astype(z_ref.dtype)


def in_proj(x2, g, w_wide, b_wide, w_narrow, b_narrow, *, tm=1024, tn=1536):
    T, D = x2.shape
    N = w_wide.shape[1]
    tm = min(tm, T)
    return pl.pallas_call(
        _in_proj_kernel,
        grid=(T // tm, N // tn),
        in_specs=[
            pl.BlockSpec((tm, D), lambda i, j: (i, 0)),
            pl.BlockSpec((1, D), lambda i, j: (0, 0)),
            pl.BlockSpec((D, tn), lambda i, j: (0, j)),
            pl.BlockSpec((1, tn), lambda i, j: (0, j)),
            pl.BlockSpec((D, LANES), lambda i, j: (0, 0)),
            pl.BlockSpec((1, LANES), lambda i, j: (0, 0)),
        ],
        out_specs=[
            pl.BlockSpec((tm, tn), lambda i, j: (i, j)),
            pl.BlockSpec((tm, LANES), lambda i, j: (i, 0)),
        ],
        out_shape=[
            jax.ShapeDtypeStruct((T, N), BF16),
            jax.ShapeDtypeStruct((T, LANES), F32),
        ],
        scratch_shapes=[pltpu.VMEM((tm, D), BF16)],
        compiler_params=_params(("parallel", "arbitrary"), 40),
        name="in_proj",
    )(x2, g, w_wide, b_wide, w_narrow, b_narrow)


def _gates_kernel(zn_ref, g_ref, *, rows):
    S = zn_ref.shape[0]
    lane = lax.broadcasted_iota(jnp.int32, (rows, LANES), 1)
    keep_raw = jnp.logical_and(lane >= LANE_ML_I, lane < LANE_ML_F)
    r = lax.broadcasted_iota(jnp.int32, (rows, rows), 0)
    c = lax.broadcasted_iota(jnp.int32, (rows, rows), 1)
    tril = jnp.where(r >= c, 1.0, 0.0).astype(F32)

    def body(i, carry):
        sl = pl.ds(pl.multiple_of(i * rows, rows), rows)
        z = zn_ref[sl, :]
        ls = _log_sigmoid(z)
        cs = jnp.dot(tril, ls, precision=lax.Precision.HIGHEST, preferred_element_type=F32) + carry
        g_ref[sl, :] = jnp.where(keep_raw, z, cs)
        return cs[rows - 1:rows, :]

    lax.fori_loop(0, S // rows, body, jnp.zeros((1, LANES), F32))


def gates(zn3, *, rows=256):
    B, S, _ = zn3.shape
    rows = min(rows, S)
    return pl.pallas_call(
        functools.partial(_gates_kernel, rows=rows),
        grid=(B,),
        in_specs=[pl.BlockSpec((None, S, LANES), lambda b: (b, 0, 0))],
        out_specs=pl.BlockSpec((None, S, LANES), lambda b: (b, 0, 0)),
        out_shape=jax.ShapeDtypeStruct((B, S, LANES), F32),
        compiler_params=_params(("parallel",), 16),
        name="gates",
    )(zn3)


def _fox_kernel(q_ref, k_ref, v_ref, g_ref, gt_ref, o_ref, *, blk):
    qi = pl.program_id(1)
    r = lax.broadcasted_iota(jnp.int32, (blk, blk), 0)
    c = lax.broadcasted_iota(jnp.int32, (blk, blk), 1)
    causal = r >= c
    dn = (((1,), (1,)), ((), ()))

    outs = []
    for h in range(FOX_HEADS):
        cols = slice(h * FOX_HEAD_DIM, (h + 1) * FOX_HEAD_DIM)
        q = q_ref[:, cols] * jnp.asarray(FOX_HEAD_DIM ** -0.5, BF16)
        fq = g_ref[:, LANE_FOX_F + h:LANE_FOX_F + h + 1]

        def step(kb, carry, masked):
            m, l, acc = carry
            ks = pl.ds(pl.multiple_of(kb * blk, blk), blk)
            k = k_ref[ks, cols]
            v = v_ref[ks, cols]
            s = lax.dot_general(q, k, dn, preferred_element_type=F32)
            s = s + fq - gt_ref[kb, LANE_FOX_F + h:LANE_FOX_F + h + 1, :]
            if masked:
                s = jnp.where(causal, s, -jnp.inf)
            m_new = jnp.maximum(m, jnp.max(s, axis=-1, keepdims=True))
            alpha = jnp.exp(m - m_new)
            p = jnp.exp(s - m_new)
            l = alpha * l + jnp.sum(p, axis=-1, keepdims=True)
            acc = alpha * acc + jnp.dot(p.astype(BF16), v, preferred_element_type=F32)
            return m_new, l, acc

        init = (jnp.full((blk, 1), -jnp.inf, F32), jnp.zeros((blk, 1), F32),
                jnp.zeros((blk, FOX_HEAD_DIM), F32))
        carry = lax.fori_loop(0, qi, functools.partial(step, masked=False), init)
        _, l, acc = step(qi, carry, True)
        outs.append(acc / l)
    o_ref[...] = jnp.concatenate(outs, axis=-1).astype(o_ref.dtype)


def fox_attention(z3, g3, gt4, *, blk=256):
    B, S, _ = z3.shape
    wq = _WIDE_OFFSETS["fox_q"] // FOX_WIDTH
    wk = _WIDE_OFFSETS["fox_k"] // FOX_WIDTH
    wv = _WIDE_OFFSETS["fox_v"] // FOX_WIDTH
    return pl.pallas_call(
        functools.partial(_fox_kernel, blk=blk),
        grid=(B, S // blk),
        in_specs=[
            pl.BlockSpec((None, blk, FOX_WIDTH), lambda b, i: (b, i, wq)),
            pl.BlockSpec((None, S, FOX_WIDTH), lambda b, i: (b, 0, wk)),
            pl.BlockSpec((None, S, FOX_WIDTH), lambda b, i: (b, 0, wv)),
            pl.BlockSpec((None, blk, LANES), lambda b, i: (b, i, 0)),
            pl.BlockSpec((None, S // blk, N_GATE_LANES, blk), lambda b, i: (b, 0, 0, 0)),
        ],
        out_specs=pl.BlockSpec((None, blk, FOX_WIDTH), lambda b, i: (b, i, 0)),
        out_shape=jax.ShapeDtypeStruct((B, S, FOX_WIDTH), BF16),
        compiler_params=_params(("parallel", "arbitrary"), 32),
        name="fox",
    )(z3, z3, z3, g3, gt4)


def _mlstm_kernel(q_ref, k_ref, v_ref, o_ref, cw_ref, gn_ref, g_ref, gt_ref, y_ref,
                  cn_ref, st_ref, halo_ref, *, L):
    S = q_ref.shape[0]
    HALO = 8
    r = lax.broadcasted_iota(jnp.int32, (L, L), 0)
    c = lax.broadcasted_iota(jnp.int32, (L, L), 1)
    tril = r >= c
    lane = lax.broadcasted_iota(jnp.int32, (L, LANES), 1)
    ones_col = jnp.where(lane == 0, 1.0, 0.0).astype(BF16)
    k_scale = ML_HEAD_DIM ** -0.5
    dn_t = (((1,), (1,)), ((), ()))

    cn_ref[...] = jnp.zeros_like(cn_ref)
    st_ref[...] = jnp.zeros_like(st_ref)
    halo_ref[...] = jnp.zeros_like(halo_ref)

    def conv_silu(x_chunk, halo, w):
        xx = jnp.concatenate([halo, x_chunk], axis=0)
        y = w[CONV_WIDTH - 1:CONV_WIDTH, :] * x_chunk
        for j in range(CONV_WIDTH - 1):
            sh = CONV_WIDTH - 1 - j
            y = y + w[j:j + 1, :] * xx[HALO - sh:HALO - sh + L, :]
        return y * jax.nn.sigmoid(y)

    def chunk(ci, _):
        rows = pl.ds(pl.multiple_of(ci * L, L), L)
        xq = q_ref[rows, :].astype(F32)
        xk = k_ref[rows, :].astype(F32)
        qa = conv_silu(xq, halo_ref[:, :ML_WIDTH], cw_ref[:, :ML_WIDTH])
        ka = conv_silu(xk, halo_ref[:, ML_WIDTH:], cw_ref[:, ML_WIDTH:]) * k_scale
        halo_ref[:, :ML_WIDTH] = xq[L - HALO:, :]
        halo_ref[:, ML_WIDTH:] = xk[L - HALO:, :]
        gcol = g_ref[rows, :]
        grow = gt_ref[ci]

        for h in range(ML_HEADS):
            cols = slice(h * ML_HEAD_DIM, (h + 1) * ML_HEAD_DIM)
            qb = qa[:, cols].astype(BF16)
            kf = ka[:, cols]
            vb = v_ref[rows, cols]
            f_c = gcol[:, LANE_ML_F + h:LANE_ML_F + h + 1]
            i_c = gcol[:, LANE_ML_I + h:LANE_ML_I + h + 1]
            f_r = grow[LANE_ML_F + h:LANE_ML_F + h + 1, :]
            i_r = grow[LANE_ML_I + h:LANE_ML_I + h + 1, :]
            f_prev = st_ref[h, 0:1, 0:1]
            m_prev = st_ref[h, 0:1, 1:2]

            d = jnp.where(tril, f_c - f_r + i_r, -jnp.inf)
            inter = f_c - f_prev + m_prev
            m = jnp.maximum(inter, jnp.max(d, axis=-1, keepdims=True))
            w_inter = jnp.exp(inter - m)
            p = lax.dot_general(qb, kf.astype(BF16), dn_t, preferred_element_type=F32) * jnp.exp(d - m)
            inter_out = jnp.dot(qb, cn_ref[h].astype(BF16), preferred_element_type=F32)
            num = w_inter * inter_out[:, :ML_HEAD_DIM] + jnp.dot(p.astype(BF16), vb, preferred_element_type=F32)
            den = w_inter * inter_out[:, ML_HEAD_DIM:ML_HEAD_DIM + 1] + jnp.sum(p, axis=-1, keepdims=True)
            hh = num / jnp.maximum(jnp.abs(den), jnp.exp(-m))

            m_new = m[L - 1:L, :]
            f_end = f_c[L - 1:L, :]
            decay = jnp.exp(f_end - f_prev + m_prev - m_new)
            w_s = jnp.exp(f_end - f_c + i_c - m_new)
            kw = (w_s * kf).astype(BF16)
            v_aug = jnp.concatenate([vb, ones_col], axis=-1)
            upd = lax.dot_general(kw, v_aug, (((0,), (0,)), ((), ())), preferred_element_type=F32)
            cn_ref[h] = decay * cn_ref[h] + upd
            st_ref[h, 0:1, 0:1] = f_end
            st_ref[h, 0:1, 1:2] = m_new

            hn = hh * lax.rsqrt(jnp.mean(hh * hh, axis=-1, keepdims=True) + EPS) * gn_ref[:, cols]
            y = jax.nn.sigmoid(o_ref[rows, cols].astype(F32)) * hn
            y_ref[rows, cols] = y.astype(y_ref.dtype)
        return 0

    lax.fori_loop(0, S // L, chunk, 0)


def mlstm(z3, conv_w, mlstm_g, g3, gt4, *, L=CHUNK):
    B, S, _ = z3.shape
    blocks = [_WIDE_OFFSETS[n] // ML_WIDTH for n in ("ml_q", "ml_k", "ml_v", "ml_o")]
    seq_spec = lambda idx: pl.BlockSpec((None, S, ML_WIDTH), lambda b: (b, 0, idx))
    return pl.pallas_call(
        functools.partial(_mlstm_kernel, L=L),
        grid=(B,),
        in_specs=[
            seq_spec(blocks[0]), seq_spec(blocks[1]), seq_spec(blocks[2]), seq_spec(blocks[3]),
            pl.BlockSpec((CONV_WIDTH, 2 * ML_WIDTH), lambda b: (0, 0)),
            pl.BlockSpec((1, ML_WIDTH), lambda b: (0, 0)),
            pl.BlockSpec((None, S, LANES), lambda b: (b, 0, 0)),
            pl.BlockSpec((None, S // L, N_GATE_LANES, L), lambda b: (b, 0, 0, 0)),
        ],
        out_specs=pl.BlockSpec((None, S, ML_WIDTH), lambda b: (b, 0, 0)),
        out_shape=jax.ShapeDtypeStruct((B, S, ML_WIDTH), BF16),
        scratch_shapes=[
            pltpu.VMEM((ML_HEADS, ML_HEAD_DIM, 2 * ML_HEAD_DIM), F32),
            pltpu.VMEM((ML_HEADS, 8, LANES), F32),
            pltpu.VMEM((8, 2 * ML_WIDTH), F32),
        ],
        compiler_params=_params(("parallel",), 40),
        name="mlstm",
    )(z3, z3, z3, z3, conv_w, mlstm_g, g3, gt4)


def _gmlp_kernel(u_ref, v_ref, gn_ref, ws_ref, bst_ref, y_ref):
    rows = u_ref.shape[0]
    u = jax.nn.gelu(u_ref[...].astype(F32))
    v = jax.nn.gelu(v_ref[...].astype(F32))
    mu = jnp.mean(v, axis=-1, keepdims=True)
    vc = v - mu
    vn = (vc * lax.rsqrt(jnp.mean(vc * vc, axis=-1, keepdims=True) + EPS) * gn_ref[...]).astype(BF16)
    r = lax.broadcasted_iota(jnp.int32, (G_SPAN, G_SPAN), 0) // CHUNK
    c = lax.broadcasted_iota(jnp.int32, (G_SPAN, G_SPAN), 1) // CHUNK
    mask = r >= c
    for g in range(G_GROUPS):
        cols = slice(g * G_GROUP_DIM, (g + 1) * G_GROUP_DIM)
        w = jnp.where(mask, ws_ref[g], 0.0).astype(BF16)
        bias = bst_ref[:, g:g + 1]
        for s in range(rows // G_SPAN):
            rs = slice(s * G_SPAN, (s + 1) * G_SPAN)
            mixed = jnp.dot(w, vn[rs, cols], preferred_element_type=F32) + bias
            y_ref[rs, cols] = (u[rs, cols] * mixed).astype(y_ref.dtype)


def gmlp(z3, gmlp_g, ws, bs_t, *, rows=512):
    B, S, _ = z3.shape
    rows = min(rows, S)
    bu = _WIDE_OFFSETS["g_u"] // G_WIDTH
    bv = _WIDE_OFFSETS["g_v"] // G_WIDTH
    return pl.pallas_call(
        _gmlp_kernel,
        grid=(B, S // rows),
        in_specs=[
            pl.BlockSpec((None, rows, G_WIDTH), lambda b, i: (b, i, bu)),
            pl.BlockSpec((None, rows, G_WIDTH), lambda b, i: (b, i, bv)),
            pl.BlockSpec((1, G_WIDTH), lambda b, i: (0, 0)),
            pl.BlockSpec((G_GROUPS, G_SPAN, G_SPAN), lambda b, i: (0, 0, 0)),
            pl.BlockSpec((G_SPAN, G_GROUPS), lambda b, i: (0, 0)),
        ],
        out_specs=pl.BlockSpec((None, rows, G_WIDTH), lambda b, i: (b, i, 0)),
        out_shape=jax.ShapeDtypeStruct((B, S, G_WIDTH), BF16),
        compiler_params=_params(("parallel", "parallel"), 16),
        name="gmlp",
    )(z3, z3, gmlp_g, ws, bs_t)


def _merge_kernel(x_ref, gate_ref, ya_ref, yb_ref, yc_ref, wb_ref, wo_ref, gpost_ref, o_ref):
    merged = None
    for n, y_ref in enumerate((ya_ref, yb_ref, yc_ref)):
        br = jnp.dot(y_ref[...], wb_ref[n], preferred_element_type=F32)
        gt = jax.nn.sigmoid(gate_ref[:, n * D_MODEL:(n + 1) * D_MODEL].astype(F32))
        merged = gt * br if merged is None else merged + gt * br
    y = jnp.dot(merged.astype(BF16), wo_ref[...], preferred_element_type=F32)
    o_ref[...] = x_ref[...] + _rms(y, gpost_ref[...])


def merge(x2, z2, y_fox, y_ml, y_g, w_branch, w_out, g_post, *, tm=512):
    T, D = x2.shape
    tm = min(tm, T)
    row = lambda w: pl.BlockSpec((tm, w), lambda i: (i, 0))
    return pl.pallas_call(
        _merge_kernel,
        grid=(T // tm,),
        in_specs=[
            row(D),
            row(N_BRANCH * D),
            row(FOX_WIDTH), row(ML_WIDTH), row(G_WIDTH),
            pl.BlockSpec((N_BRANCH, FOX_WIDTH, D), lambda i: (0, 0, 0)),
            pl.BlockSpec((D, D), lambda i: (0, 0)),
            pl.BlockSpec((1, D), lambda i: (0, 0)),
        ],
        out_specs=row(D),
        out_shape=jax.ShapeDtypeStruct((T, D), F32),
        compiler_params=_params(("parallel",), 48),
        name="merge",
    )(x2, z2, y_fox, y_ml, y_g, w_branch, w_out, g_post)


def _norm_matmul_kernel(x_ref, g_ref, w_ref, o_ref):
    h = _rms(x_ref[...], g_ref[...]).astype(BF16)
    o_ref[...] = jnp.dot(h, w_ref[...], preferred_element_type=F32).astype(o_ref.dtype)


def norm_matmul(x2, g, w, *, tm=512):
    T, D = x2.shape
    N = w.shape[1]
    tm = min(tm, T)
    return pl.pallas_call(
        _norm_matmul_kernel,
        grid=(T // tm,),
        in_specs=[
            pl.BlockSpec((tm, D), lambda i: (i, 0)),
            pl.BlockSpec((1, D), lambda i: (0, 0)),
            pl.BlockSpec((D, N), lambda i: (0, 0)),
        ],
        out_specs=pl.BlockSpec((tm, N), lambda i: (i, 0)),
        out_shape=jax.ShapeDtypeStruct((T, N), BF16),
        compiler_params=_params(("parallel",), 40),
        name="mem_kv",
    )(x2, g, w)


def _xattn_kernel(x_ref, kv_ref, wq_ref, wo_ref, gpre_ref, gpost_ref, o_ref):
    x = x_ref[...]
    h = _rms(x, gpre_ref[...]).astype(BF16)
    q = jnp.dot(h, wq_ref[...], preferred_element_type=F32).astype(BF16)
    dn = (((1,), (1,)), ((), ()))
    outs = []
    for hd in range(X_HEADS):
        cols = slice(hd * X_HEAD_DIM, (hd + 1) * X_HEAD_DIM)
        k = kv_ref[:, cols]
        v = kv_ref[:, D_MODEL + hd * X_HEAD_DIM:D_MODEL + (hd + 1) * X_HEAD_DIM]
        s = lax.dot_general(q[:, cols], k, dn, preferred_element_type=F32) * (X_HEAD_DIM ** -0.5)
        s = s - jnp.max(s, axis=-1, keepdims=True)
        e = jnp.exp(s)
        p = e / jnp.sum(e, axis=-1, keepdims=True)
        outs.append(jnp.dot(p.astype(BF16), v, preferred_element_type=F32).astype(BF16))
    o = jnp.concatenate(outs, axis=-1)
    y = jnp.dot(o, wo_ref[...], preferred_element_type=F32)
    o_ref[...] = x + _rms(y, gpost_ref[...])


def xattn(x3, kv3, w_q, w_o, g_pre, g_post, *, tm=512):
    B, S, D = x3.shape
    M = kv3.shape[1]
    tm = min(tm, S)
    const = lambda shape: pl.BlockSpec(shape, lambda b, i: (0,) * len(shape))
    return pl.pallas_call(
        _xattn_kernel,
        grid=(B, S // tm),
        in_specs=[
            pl.BlockSpec((None, tm, D), lambda b, i: (b, i, 0)),
            pl.BlockSpec((None, M, 2 * D), lambda b, i: (b, 0, 0)),
            const((D, D)), const((D, D)), const((1, D)), const((1, D)),
        ],
        out_specs=pl.BlockSpec((None, tm, D), lambda b, i: (b, i, 0)),
        out_shape=jax.ShapeDtypeStruct((B, S, D), F32),
        compiler_params=_params(("parallel", "arbitrary"), 48),
        name="xattn",
    )(x3, kv3, w_q, w_o, g_pre, g_post)


def _ffn_kernel(x_ref, w1_ref, w2_ref, gpre_ref, gpost_ref, o_ref, *, ff_chunk):
    x = x_ref[...]
    h = _rms(x, gpre_ref[...]).astype(BF16)
    acc = None
    for c in range(w1_ref.shape[1] // ff_chunk):
        cs = slice(c * ff_chunk, (c + 1) * ff_chunk)
        a = jnp.dot(h, w1_ref[:, cs], preferred_element_type=F32)
        a = jnp.square(jnp.maximum(a, 0.0)).astype(BF16)
        part = jnp.dot(a, w2_ref[cs, :], preferred_element_type=F32)
        acc = part if acc is None else acc + part
    o_ref[...] = x + _rms(acc, gpost_ref[...])


def ffn(x2, w1, w2, g_pre, g_post, *, tm=512, ff_chunk=1024):
    T, D = x2.shape
    FF = w1.shape[1]
    tm = min(tm, T)
    return pl.pallas_call(
        functools.partial(_ffn_kernel, ff_chunk=ff_chunk),
        grid=(T // tm,),
        in_specs=[
            pl.BlockSpec((tm, D), lambda i: (i, 0)),
            pl.BlockSpec((D, FF), lambda i: (0, 0)),
            pl.BlockSpec((FF, D), lambda i: (0, 0)),
            pl.BlockSpec((1, D), lambda i: (0, 0)),
            pl.BlockSpec((1, D), lambda i: (0, 0)),
        ],
        out_specs=pl.BlockSpec((tm, D), lambda i: (i, 0)),
        out_shape=jax.ShapeDtypeStruct((T, D), F32),
        compiler_params=_params(("parallel",), 56),
        name="ffn",
    )(x2, w1, w2, g_pre, g_post)


def _regroup_in_proj(w_in, b_in):
    def cols(names):
        return [slice(_IN_OFFSETS[n][0], _IN_OFFSETS[n][0] + _IN_OFFSETS[n][1]) for n in names]
    w_wide = jnp.concatenate([w_in[:, s] for s in cols(_WIDE_ORDER)], axis=1).astype(BF16)
    b_wide = jnp.concatenate([b_in[s] for s in cols(_WIDE_ORDER)])[None, :]
    pad = LANES - N_GATE_LANES
    w_narrow = jnp.pad(jnp.concatenate([w_in[:, s] for s in cols(_NARROW_ORDER)], axis=1),
                       ((0, 0), (0, pad))).astype(BF16)
    b_narrow = jnp.pad(jnp.concatenate([b_in[s] for s in cols(_NARROW_ORDER)]), (0, pad))[None, :]
    return w_wide, b_wide, w_narrow, b_narrow


def _time_on_lanes(g3, blk):
    B, S, _ = g3.shape
    return g3[:, :, :N_GATE_LANES].reshape(B, S // blk, blk, N_GATE_LANES).transpose(0, 1, 3, 2)


def _layer(x3, mem2, norms, w_in, b_in, conv_w, mlstm_g, gmlp_g, gmlp_ws, gmlp_bs,
           w_branch, w_out, w_xq, w_xkv, w_xo, w_ff1, w_ff2, *, fox_blk, ml_chunk):
    B, S, D = x3.shape
    T = B * S
    M = mem2.shape[0] // B
    g = lambda idx: norms[idx][None, :]
    x2 = x3.reshape(T, D)

    w_wide, b_wide, w_narrow, b_narrow = _regroup_in_proj(w_in, b_in)
    z2, zn2 = in_proj(x2, g(0), w_wide, b_wide, w_narrow, b_narrow)
    z3 = z2.reshape(B, S, N_WIDE)
    g3 = gates(zn2.reshape(B, S, LANES))

    y_fox = fox_attention(z3, g3, _time_on_lanes(g3, fox_blk), blk=fox_blk)
    y_ml = mlstm(z3, conv_w, mlstm_g[None, :], g3, _time_on_lanes(g3, ml_chunk), L=ml_chunk)
    y_g = gmlp(z3, gmlp_g[None, :], gmlp_ws, gmlp_bs.T)
    x2 = merge(x2, z2, y_fox.reshape(T, FOX_WIDTH), y_ml.reshape(T, ML_WIDTH), y_g.reshape(T, G_WIDTH),
               w_branch.astype(BF16), w_out.astype(BF16), g(1))

    kv = norm_matmul(mem2, g(4), w_xkv.astype(BF16))
    x3 = xattn(x2.reshape(B, S, D), kv.reshape(B, M, 2 * D), w_xq.astype(BF16), w_xo.astype(BF16), g(2), g(3))

    x2 = ffn(x3.reshape(T, D), w_ff1.astype(BF16), w_ff2.astype(BF16), g(5), g(6))
    return x2.reshape(B, S, D)


def kernel(x, mem, norms, w_in, b_in, conv_w, mlstm_norm, gmlp_norm, gmlp_ws, gmlp_bs, w_branch, w_out, w_xq, w_xkv, w_xo, w_ff1, w_ff2):
    B, M, D = mem.shape
    S = x.shape[1]
    mem2 = mem.reshape(B * M, D)
    fox_blk = min(256, S)
    for l in range(norms.shape[0]):
        x = _layer(x, mem2, norms[l], w_in[l], b_in[l], conv_w[l], mlstm_norm[l], gmlp_norm[l],
                   gmlp_ws[l], gmlp_bs[l], w_branch[l], w_out[l], w_xq[l], w_xkv[l], w_xo[l],
                   w_ff1[l], w_ff2[l], fox_blk=fox_blk, ml_chunk=CHUNK)
    return x
```

```python
import functools

import jax
import jax.numpy as jnp
import numpy as np
from jax import lax
from jax.experimental import pallas as pl
from jax.experimental.pallas import tpu as pltpu

F32 = jnp.float32
BF16 = jnp.bfloat16

EPS = 1e-6
D_MODEL = 1024
CHUNK = 64
FOX_HEADS = 8
FOX_HEAD_DIM = 64
FOX_WIDTH = FOX_HEADS * FOX_HEAD_DIM
ML_HEADS = 4
ML_HEAD_DIM = 128
ML_WIDTH = ML_HEADS * ML_HEAD_DIM
ML_CHUNK = 128
CONV_WIDTH = 4
G_GROUPS = 4
G_GROUP_DIM = 128
G_WIDTH = G_GROUPS * G_GROUP_DIM
G_SPAN = 128
N_BRANCH = 3
X_HEADS = 4
X_HEAD_DIM = D_MODEL // X_HEADS
D_FF = 4 * D_MODEL

LANES = 128
MIB = 1024 * 1024

_IN_SPLITS = (
    ("fox_q", FOX_WIDTH), ("fox_k", FOX_WIDTH), ("fox_v", FOX_WIDTH), ("fox_f", FOX_HEADS),
    ("ml_q", ML_WIDTH), ("ml_k", ML_WIDTH), ("ml_v", ML_WIDTH),
    ("ml_i", ML_HEADS), ("ml_f", ML_HEADS), ("ml_o", ML_WIDTH),
    ("g_u", G_WIDTH), ("g_v", G_WIDTH),
    ("gate", N_BRANCH * D_MODEL),
)
_IN_OFFSETS = {}
_off = 0
for _name, _size in _IN_SPLITS:
    _IN_OFFSETS[_name] = (_off, _size)
    _off += _size

_WIDE_ORDER = ("gate", "fox_q", "fox_k", "fox_v", "ml_q", "ml_k", "ml_v", "ml_o", "g_u", "g_v")
_WIDE_OFFSETS = {}
_off = 0
for _name in _WIDE_ORDER:
    _WIDE_OFFSETS[_name] = _off
    _off += _IN_OFFSETS[_name][1]
N_WIDE = _off
_NARROW_ORDER = ("fox_f", "ml_i", "ml_f")
LANE_FOX_F = 0
LANE_ML_I = FOX_HEADS
LANE_ML_F = FOX_HEADS + ML_HEADS
N_GATE_LANES = FOX_HEADS + 2 * ML_HEADS


def _rms(x, g):
    return x * lax.rsqrt(jnp.mean(x * x, axis=-1, keepdims=True) + EPS) * g


def _log_sigmoid(x):
    return jnp.minimum(x, 0.0) - jnp.log1p(jnp.exp(-jnp.abs(x)))


def _params(semantics, vmem_mib):
    return pltpu.CompilerParams(dimension_semantics=semantics, vmem_limit_bytes=vmem_mib * MIB)


def _in_proj_kernel(x_ref, g_ref, w_ref, b_ref, wn_ref, bn_ref, z_ref, zn_ref, h_ref):
    @pl.when(pl.program_id(1) == 0)
    def _():
        h = _rms(x_ref[...], g_ref[...]).astype(BF16)
        h_ref[...] = h
        zn_ref[...] = jnp.dot(h, wn_ref[...], preferred_element_type=F32) + bn_ref[...]

    z = jnp.dot(h_ref[...], w_ref[...], preferred_element_type=F32) + b_ref[...]
    z_ref[...] = z.astype(z_ref.dtype)


def in_proj(x2, g, w_wide, b_wide, w_narrow, b_narrow, *, tm=1024, tn=1536):
    T, D = x2.shape
    N = w_wide.shape[1]
    tm = min(tm, T)
    return pl.pallas_call(
        _in_proj_kernel,
        grid=(T // tm, N // tn),
        in_specs=[
            pl.BlockSpec((tm, D), lambda i, j: (i, 0)),
            pl.BlockSpec((1, D), lambda i, j: (0, 0)),
            pl.BlockSpec((D, tn), lambda i, j: (0, j)),
            pl.BlockSpec((1, tn), lambda i, j: (0, j)),
            pl.BlockSpec((D, LANES), lambda i, j: (0, 0)),
            pl.BlockSpec((1, LANES), lambda i, j: (0, 0)),
        ],
        out_specs=[
            pl.BlockSpec((tm, tn), lambda i, j: (i, j)),
            pl.BlockSpec((tm, LANES), lambda i, j: (i, 0)),
        ],
        out_shape=[
            jax.ShapeDtypeStruct((T, N), BF16),
            jax.ShapeDtypeStruct((T, LANES), F32),
        ],
        scratch_shapes=[pltpu.VMEM((tm, D), BF16)],
        compiler_params=_params(("parallel", "arbitrary"), 40),
        name="in_proj",
    )(x2, g, w_wide, b_wide, w_narrow, b_narrow)


def _gates_kernel(zn_ref, g_ref, *, rows):
    S = zn_ref.shape[0]
    lane = lax.broadcasted_iota(jnp.int32, (rows, LANES), 1)
    keep_raw = jnp.logical_and(lane >= LANE_ML_I, lane < LANE_ML_F)
    r = lax.broadcasted_iota(jnp.int32, (rows, rows), 0)
    c = lax.broadcasted_iota(jnp.int32, (rows, rows), 1)
    tril = jnp.where(r >= c, 1.0, 0.0).astype(F32)

    def body(i, carry):
        sl = pl.ds(pl.multiple_of(i * rows, rows), rows)
        z = zn_ref[sl, :]
        ls = _log_sigmoid(z)
        cs = jnp.dot(tril, ls, precision=lax.Precision.HIGHEST, preferred_element_type=F32) + carry
        g_ref[sl, :] = jnp.where(keep_raw, z, cs)
        return cs[rows - 1:rows, :]

    lax.fori_loop(0, S // rows, body, jnp.zeros((1, LANES), F32))


def gates(zn3, *, rows=256):
    B, S, _ = zn3.shape
    rows = min(rows, S)
    return pl.pallas_call(
        functools.partial(_gates_kernel, rows=rows),
        grid=(B,),
        in_specs=[pl.BlockSpec((None, S, LANES), lambda b: (b, 0, 0))],
        out_specs=pl.BlockSpec((None, S, LANES), lambda b: (b, 0, 0)),
        out_shape=jax.ShapeDtypeStruct((B, S, LANES), F32),
        compiler_params=_params(("parallel",), 16),
        name="gates",
    )(zn3)


FOX_AUG = LANES
FOX_PAIR = 2 * FOX_HEAD_DIM
FOX_VT_ROWS = FOX_HEAD_DIM + 16


def _fox_place_matrices():
    pk = np.zeros((3 * LANES, FOX_HEADS * FOX_AUG), np.float32)
    pq = np.zeros((3 * LANES, FOX_HEADS * FOX_AUG), np.float32)
    ck = np.zeros((1, FOX_HEADS * FOX_AUG), np.float32)
    cq = np.zeros((1, FOX_HEADS * FOX_AUG), np.float32)
    for h in range(FOX_HEADS):
        for piece in range(3):
            pk[piece * LANES + LANE_FOX_F + h, h * FOX_AUG + piece] = -1.0
            pq[piece * LANES + LANE_FOX_F + h, h * FOX_AUG + 3 + piece] = 1.0
            ck[0, h * FOX_AUG + 3 + piece] = 1.0
            cq[0, h * FOX_AUG + piece] = 1.0
    return (jnp.asarray(pk, BF16), jnp.asarray(pq, BF16), jnp.asarray(ck), jnp.asarray(cq))


def _fox_bias_lanes(f, place_ref, ones_ref):
    hi = f.astype(BF16)
    r1 = f - hi.astype(F32)
    mid = r1.astype(BF16)
    lo = (r1 - mid.astype(F32)).astype(BF16)
    x = jnp.concatenate([hi, mid, lo], axis=-1)
    return (jnp.dot(x, place_ref[...], preferred_element_type=F32) + ones_ref[...]).astype(BF16)


def _fox_kernel(q_ref, k_ref, vt_ref, g_ref, pk_ref, pq_ref, ck_ref, cq_ref, o_ref,
                kaug_ref, qaug_ref, m_ref, acc_ref, al_ref, st_ref, p_ref, *, blk):
    S = k_ref.shape[0]
    qi = pl.program_id(1)
    key_pos = lax.broadcasted_iota(jnp.int32, (blk, blk), 0)
    qry_pos = lax.broadcasted_iota(jnp.int32, (blk, blk), 1)
    causal = key_pos <= qry_pos
    dn = (((1,), (1,)), ((), ()))

    @pl.when(qi == 0)
    def _():
        for c in range(S // blk):
            rows = slice(c * blk, (c + 1) * blk)
            kaug_ref[rows, :] = _fox_bias_lanes(g_ref[rows, :], pk_ref, ck_ref)

    q_rows = pl.ds(pl.multiple_of(qi * blk, blk), blk)
    qaug = _fox_bias_lanes(g_ref[q_rows, :], pq_ref, cq_ref)
    half = lax.broadcasted_iota(jnp.int32, (blk, FOX_PAIR), 1) // FOX_HEAD_DIM
    scale = jnp.asarray(FOX_HEAD_DIM ** -0.5, BF16)
    for h in range(FOX_HEADS):
        pair = slice((h // 2) * FOX_PAIR, (h // 2 + 1) * FOX_PAIR)
        qm = jnp.where(half == h % 2, q_ref[:, pair] * scale, jnp.zeros((), BF16))
        qaug_ref[h] = jnp.concatenate([qm, qaug[:, h * FOX_AUG:(h + 1) * FOX_AUG]], axis=-1).T
    m_ref[...] = jnp.full_like(m_ref, -jnp.inf)
    acc_ref[...] = jnp.zeros_like(acc_ref)

    def step(kb, masked):
        ks = pl.ds(pl.multiple_of(kb * blk, blk), blk)
        for h in range(FOX_HEADS):
            pair = slice((h // 2) * FOX_PAIR, (h // 2 + 1) * FOX_PAIR)
            kk = jnp.concatenate([k_ref[ks, pair], kaug_ref[ks, h * FOX_AUG:(h + 1) * FOX_AUG]], axis=-1)
            st_ref[h] = jnp.dot(kk, qaug_ref[h], preferred_element_type=F32)
        for h in range(FOX_HEADS):
            st = st_ref[h]
            if masked:
                st = jnp.where(causal, st, -jnp.inf)
            m_old = m_ref[h:h + 1, :]
            m_new = jnp.maximum(m_old, jnp.max(st, axis=0, keepdims=True))
            al_ref[h:h + 1, :] = jnp.exp(m_old - m_new)
            m_ref[h:h + 1, :] = m_new
            p_ref[h] = jnp.exp(st - m_new).astype(BF16)
        for h in range(FOX_HEADS):
            pv = jnp.dot(vt_ref[kb, h], p_ref[h], preferred_element_type=F32)
            acc_ref[h] = al_ref[h:h + 1, :] * acc_ref[h] + pv

    def body(kb, carry):
        step(kb, False)
        return carry

    lax.fori_loop(0, qi, body, 0)
    step(qi, True)
    outs = []
    for j in range(FOX_HEADS // 2):
        tops = []
        for h in (2 * j, 2 * j + 1):
            a = acc_ref[h]
            tops.append(a[:FOX_HEAD_DIM, :] / a[FOX_HEAD_DIM:FOX_HEAD_DIM + 1, :])
        outs.append(jnp.concatenate(tops, axis=0).T)
    o_ref[...] = jnp.concatenate(outs, axis=-1).astype(o_ref.dtype)


def _fox_values_on_rows(z3, blk):
    B, S, _ = z3.shape
    off = _WIDE_OFFSETS["fox_v"]
    v = z3[:, :, off:off + FOX_WIDTH].reshape(B, S // blk, blk, FOX_HEADS, FOX_HEAD_DIM)
    vt = v.transpose(0, 1, 3, 4, 2)
    ones = jnp.ones((B, S // blk, FOX_HEADS, FOX_VT_ROWS - FOX_HEAD_DIM, blk), BF16)
    return jnp.concatenate([vt, ones], axis=3)


def fox_attention(z3, g3, *, blk=256):
    B, S, _ = z3.shape
    wq = _WIDE_OFFSETS["fox_q"] // FOX_WIDTH
    wk = _WIDE_OFFSETS["fox_k"] // FOX_WIDTH
    vt5 = _fox_values_on_rows(z3, blk)
    pk, pq, ck, cq = _fox_place_matrices()
    const = lambda a: pl.BlockSpec(a.shape, lambda b, i: (0,) * a.ndim)
    return pl.pallas_call(
        functools.partial(_fox_kernel, blk=blk),
        grid=(B, S // blk),
        in_specs=[
            pl.BlockSpec((None, blk, FOX_WIDTH), lambda b, i: (b, i, wq)),
            pl.BlockSpec((None, S, FOX_WIDTH), lambda b, i: (b, 0, wk)),
            pl.BlockSpec((None, S // blk, FOX_HEADS, FOX_VT_ROWS, blk), lambda b, i: (b, 0, 0, 0, 0)),
            pl.BlockSpec((None, S, LANES), lambda b, i: (b, 0, 0)),
            const(pk), const(pq), const(ck), const(cq),
        ],
        out_specs=pl.BlockSpec((None, blk, FOX_WIDTH), lambda b, i: (b, i, 0)),
        out_shape=jax.ShapeDtypeStruct((B, S, FOX_WIDTH), BF16),
        scratch_shapes=[
            pltpu.VMEM((S, FOX_HEADS * FOX_AUG), BF16),
            pltpu.VMEM((FOX_HEADS, FOX_PAIR + FOX_AUG, blk), BF16),
            pltpu.VMEM((FOX_HEADS, blk), F32),
            pltpu.VMEM((FOX_HEADS, FOX_VT_ROWS, blk), F32),
            pltpu.VMEM((FOX_HEADS, blk), F32),
            pltpu.VMEM((FOX_HEADS, blk, blk), F32),
            pltpu.VMEM((FOX_HEADS, blk, blk), BF16),
        ],
        compiler_params=_params(("parallel", "arbitrary"), 40),
        name="fox",
    )(z3, z3, vt5, g3, pk, pq, ck, cq)


def _mlstm_kernel(q_ref, k_ref, v_ref, o_ref, cw_ref, gn_ref, g_ref, gt_ref, y_ref,
                  cn_ref, st_ref, halo_ref, *, L):
    S = q_ref.shape[0]
    HALO = 8
    r = lax.broadcasted_iota(jnp.int32, (L, L), 0)
    c = lax.broadcasted_iota(jnp.int32, (L, L), 1)
    tril = r >= c
    lane = lax.broadcasted_iota(jnp.int32, (L, LANES), 1)
    ones_col = jnp.where(lane == 0, 1.0, 0.0).astype(BF16)
    k_scale = ML_HEAD_DIM ** -0.5
    dn_t = (((1,), (1,)), ((), ()))

    cn_ref[...] = jnp.zeros_like(cn_ref)
    st_ref[...] = jnp.zeros_like(st_ref)
    halo_ref[...] = jnp.zeros_like(halo_ref)

    def conv_silu(x_chunk, halo, w):
        xx = jnp.concatenate([halo, x_chunk], axis=0)
        y = w[CONV_WIDTH - 1:CONV_WIDTH, :] * x_chunk
        for j in range(CONV_WIDTH - 1):
            sh = CONV_WIDTH - 1 - j
            y = y + w[j:j + 1, :] * xx[HALO - sh:HALO - sh + L, :]
        return y * jax.nn.sigmoid(y)

    def chunk(ci, _):
        rows = pl.ds(pl.multiple_of(ci * L, L), L)
        xq = q_ref[rows, :].astype(F32)
        xk = k_ref[rows, :].astype(F32)
        qa = conv_silu(xq, halo_ref[:, :ML_WIDTH], cw_ref[:, :ML_WIDTH])
        ka = conv_silu(xk, halo_ref[:, ML_WIDTH:], cw_ref[:, ML_WIDTH:]) * k_scale
        halo_ref[:, :ML_WIDTH] = xq[L - HALO:, :]
        halo_ref[:, ML_WIDTH:] = xk[L - HALO:, :]
        gcol = g_ref[rows, :]
        grow = gt_ref[ci]

        for h in range(ML_HEADS):
            cols = slice(h * ML_HEAD_DIM, (h + 1) * ML_HEAD_DIM)
            qb = qa[:, cols].astype(BF16)
            kf = ka[:, cols]
            vb = v_ref[rows, cols]
            f_c = gcol[:, LANE_ML_F + h:LANE_ML_F + h + 1]
            i_c = gcol[:, LANE_ML_I + h:LANE_ML_I + h + 1]
            f_r = grow[LANE_ML_F + h:LANE_ML_F + h + 1, :]
            i_r = grow[LANE_ML_I + h:LANE_ML_I + h + 1, :]
            f_prev = st_ref[h, 0:1, 0:1]
            m_prev = st_ref[h, 0:1, 1:2]

            d = jnp.where(tril, f_c - f_r + i_r, -jnp.inf)
            inter = f_c - f_prev + m_prev
            m = jnp.maximum(inter, jnp.max(d, axis=-1, keepdims=True))
            w_inter = jnp.exp(inter - m)
            p = lax.dot_general(qb, kf.astype(BF16), dn_t, preferred_element_type=F32) * jnp.exp(d - m)
            inter_out = jnp.dot(qb, cn_ref[h].astype(BF16), preferred_element_type=F32)
            num = w_inter * inter_out[:, :ML_HEAD_DIM] + jnp.dot(p.astype(BF16), vb, preferred_element_type=F32)
            den = w_inter * inter_out[:, ML_HEAD_DIM:ML_HEAD_DIM + 1] + jnp.sum(p, axis=-1, keepdims=True)
            hh = num / jnp.maximum(jnp.abs(den), jnp.exp(-m))

            m_new = m[L - 1:L, :]
            f_end = f_c[L - 1:L, :]
            decay = jnp.exp(f_end - f_prev + m_prev - m_new)
            w_s = jnp.exp(f_end - f_c + i_c - m_new)
            kw = (w_s * kf).astype(BF16)
            v_aug = jnp.concatenate([vb, ones_col], axis=-1)
            upd = lax.dot_general(kw, v_aug, (((0,), (0,)), ((), ())), preferred_element_type=F32)
            cn_ref[h] = decay * cn_ref[h] + upd
            st_ref[h, 0:1, 0:1] = f_end
            st_ref[h, 0:1, 1:2] = m_new

            hn = hh * lax.rsqrt(jnp.mean(hh * hh, axis=-1, keepdims=True) + EPS) * gn_ref[:, cols]
            y = jax.nn.sigmoid(o_ref[rows, cols].astype(F32)) * hn
            y_ref[rows, cols] = y.astype(y_ref.dtype)
        return 0

    lax.fori_loop(0, S // L, chunk, 0)


def mlstm(z3, conv_w, mlstm_g, g3, gt4, *, L=CHUNK):
    B, S, _ = z3.shape
    blocks = [_WIDE_OFFSETS[n] // ML_WIDTH for n in ("ml_q", "ml_k", "ml_v", "ml_o")]
    seq_spec = lambda idx: pl.BlockSpec((None, S, ML_WIDTH), lambda b: (b, 0, idx))
    return pl.pallas_call(
        functools.partial(_mlstm_kernel, L=L),
        grid=(B,),
        in_specs=[
            seq_spec(blocks[0]), seq_spec(blocks[1]), seq_spec(blocks[2]), seq_spec(blocks[3]),
            pl.BlockSpec((CONV_WIDTH, 2 * ML_WIDTH), lambda b: (0, 0)),
            pl.BlockSpec((1, ML_WIDTH), lambda b: (0, 0)),
            pl.BlockSpec((None, S, LANES), lambda b: (b, 0, 0)),
            pl.BlockSpec((None, S // L, N_GATE_LANES, L), lambda b: (b, 0, 0, 0)),
        ],
        out_specs=pl.BlockSpec((None, S, ML_WIDTH), lambda b: (b, 0, 0)),
        out_shape=jax.ShapeDtypeStruct((B, S, ML_WIDTH), BF16),
        scratch_shapes=[
            pltpu.VMEM((ML_HEADS, ML_HEAD_DIM, 2 * ML_HEAD_DIM), F32),
            pltpu.VMEM((ML_HEADS, 8, LANES), F32),
            pltpu.VMEM((8, 2 * ML_WIDTH), F32),
        ],
        compiler_params=_params(("parallel",), 40),
        name="mlstm",
    )(z3, z3, z3, z3, conv_w, mlstm_g, g3, gt4)


def _gmlp_kernel(u_ref, v_ref, gn_ref, ws_ref, bst_ref, y_ref):
    rows = u_ref.shape[0]
    u = jax.nn.gelu(u_ref[...].astype(F32))
    v = jax.nn.gelu(v_ref[...].astype(F32))
    mu = jnp.mean(v, axis=-1, keepdims=True)
    vc = v - mu
    vn = (vc * lax.rsqrt(jnp.mean(vc * vc, axis=-1, keepdims=True) + EPS) * gn_ref[...]).astype(BF16)
    r = lax.broadcasted_iota(jnp.int32, (G_SPAN, G_SPAN), 0) // CHUNK
    c = lax.broadcasted_iota(jnp.int32, (G_SPAN, G_SPAN), 1) // CHUNK
    mask = r >= c
    for g in range(G_GROUPS):
        cols = slice(g * G_GROUP_DIM, (g + 1) * G_GROUP_DIM)
        w = jnp.where(mask, ws_ref[g], 0.0).astype(BF16)
        bias = bst_ref[:, g:g + 1]
        for s in range(rows // G_SPAN):
            rs = slice(s * G_SPAN, (s + 1) * G_SPAN)
            mixed = jnp.dot(w, vn[rs, cols], preferred_element_type=F32) + bias
            y_ref[rs, cols] = (u[rs, cols] * mixed).astype(y_ref.dtype)


def gmlp(z3, gmlp_g, ws, bs_t, *, rows=512):
    B, S, _ = z3.shape
    rows = min(rows, S)
    bu = _WIDE_OFFSETS["g_u"] // G_WIDTH
    bv = _WIDE_OFFSETS["g_v"] // G_WIDTH
    return pl.pallas_call(
        _gmlp_kernel,
        grid=(B, S // rows),
        in_specs=[
            pl.BlockSpec((None, rows, G_WIDTH), lambda b, i: (b, i, bu)),
            pl.BlockSpec((None, rows, G_WIDTH), lambda b, i: (b, i, bv)),
            pl.BlockSpec((1, G_WIDTH), lambda b, i: (0, 0)),
            pl.BlockSpec((G_GROUPS, G_SPAN, G_SPAN), lambda b, i: (0, 0, 0)),
            pl.BlockSpec((G_SPAN, G_GROUPS), lambda b, i: (0, 0)),
        ],
        out_specs=pl.BlockSpec((None, rows, G_WIDTH), lambda b, i: (b, i, 0)),
        out_shape=jax.ShapeDtypeStruct((B, S, G_WIDTH), BF16),
        compiler_params=_params(("parallel", "parallel"), 16),
        name="gmlp",
    )(z3, z3, gmlp_g, ws, bs_t)


def _merge_kernel(x_ref, gate_ref, ya_ref, yb_ref, yc_ref, wb_ref, wo_ref, gpost_ref, o_ref):
    merged = None
    for n, y_ref in enumerate((ya_ref, yb_ref, yc_ref)):
        br = jnp.dot(y_ref[...], wb_ref[n], preferred_element_type=F32)
        gt = jax.nn.sigmoid(gate_ref[:, n * D_MODEL:(n + 1) * D_MODEL].astype(F32))
        merged = gt * br if merged is None else merged + gt * br
    y = jnp.dot(merged.astype(BF16), wo_ref[...], preferred_element_type=F32)
    o_ref[...] = x_ref[...] + _rms(y, gpost_ref[...])


def merge(x2, z2, y_fox, y_ml, y_g, w_branch, w_out, g_post, *, tm=512):
    T, D = x2.shape
    tm = min(tm, T)
    row = lambda w: pl.BlockSpec((tm, w), lambda i: (i, 0))
    return pl.pallas_call(
        _merge_kernel,
        grid=(T // tm,),
        in_specs=[
            row(D),
            row(N_BRANCH * D),
            row(FOX_WIDTH), row(ML_WIDTH), row(G_WIDTH),
            pl.BlockSpec((N_BRANCH, FOX_WIDTH, D), lambda i: (0, 0, 0)),
            pl.BlockSpec((D, D), lambda i: (0, 0)),
            pl.BlockSpec((1, D), lambda i: (0, 0)),
        ],
        out_specs=row(D),
        out_shape=jax.ShapeDtypeStruct((T, D), F32),
        compiler_params=_params(("parallel",), 48),
        name="merge",
    )(x2, z2, y_fox, y_ml, y_g, w_branch, w_out, g_post)


def _norm_matmul_kernel(x_ref, g_ref, w_ref, o_ref):
    h = _rms(x_ref[...], g_ref[...]).astype(BF16)
    o_ref[...] = jnp.dot(h, w_ref[...], preferred_element_type=F32).astype(o_ref.dtype)


def norm_matmul(x2, g, w, *, tm=512):
    T, D = x2.shape
    N = w.shape[1]
    tm = min(tm, T)
    return pl.pallas_call(
        _norm_matmul_kernel,
        grid=(T // tm,),
        in_specs=[
            pl.BlockSpec((tm, D), lambda i: (i, 0)),
            pl.BlockSpec((1, D), lambda i: (0, 0)),
            pl.BlockSpec((D, N), lambda i: (0, 0)),
        ],
        out_specs=pl.BlockSpec((tm, N), lambda i: (i, 0)),
        out_shape=jax.ShapeDtypeStruct((T, N), BF16),
        compiler_params=_params(("parallel",), 40),
        name="mem_kv",
    )(x2, g, w)


def _xattn_kernel(x_ref, kv_ref, wq_ref, wo_ref, gpre_ref, gpost_ref, o_ref):
    x = x_ref[...]
    h = _rms(x, gpre_ref[...]).astype(BF16)
    q = jnp.dot(h, wq_ref[...], preferred_element_type=F32).astype(BF16)
    dn = (((1,), (1,)), ((), ()))
    outs = []
    for hd in range(X_HEADS):
        cols = slice(hd * X_HEAD_DIM, (hd + 1) * X_HEAD_DIM)
        k = kv_ref[:, cols]
        v = kv_ref[:, D_MODEL + hd * X_HEAD_DIM:D_MODEL + (hd + 1) * X_HEAD_DIM]
        s = lax.dot_general(q[:, cols], k, dn, preferred_element_type=F32) * (X_HEAD_DIM ** -0.5)
        s = s - jnp.max(s, axis=-1, keepdims=True)
        e = jnp.exp(s)
        p = e / jnp.sum(e, axis=-1, keepdims=True)
        outs.append(jnp.dot(p.astype(BF16), v, preferred_element_type=F32).astype(BF16))
    o = jnp.concatenate(outs, axis=-1)
    y = jnp.dot(o, wo_ref[...], preferred_element_type=F32)
    o_ref[...] = x + _rms(y, gpost_ref[...])


def xattn(x3, kv3, w_q, w_o, g_pre, g_post, *, tm=512):
    B, S, D = x3.shape
    M = kv3.shape[1]
    tm = min(tm, S)
    const = lambda shape: pl.BlockSpec(shape, lambda b, i: (0,) * len(shape))
    return pl.pallas_call(
        _xattn_kernel,
        grid=(B, S // tm),
        in_specs=[
            pl.BlockSpec((None, tm, D), lambda b, i: (b, i, 0)),
            pl.BlockSpec((None, M, 2 * D), lambda b, i: (b, 0, 0)),
            const((D, D)), const((D, D)), const((1, D)), const((1, D)),
        ],
        out_specs=pl.BlockSpec((None, tm, D), lambda b, i: (b, i, 0)),
        out_shape=jax.ShapeDtypeStruct((B, S, D), F32),
        compiler_params=_params(("parallel", "arbitrary"), 48),
        name="xattn",
    )(x3, kv3, w_q, w_o, g_pre, g_post)


def _ffn_kernel(x_ref, w1_ref, w2_ref, gpre_ref, gpost_ref, o_ref, *, ff_chunk):
    x = x_ref[...]
    h = _rms(x, gpre_ref[...]).astype(BF16)
    acc = None
    for c in range(w1_ref.shape[1] // ff_chunk):
        cs = slice(c * ff_chunk, (c + 1) * ff_chunk)
        a = jnp.dot(h, w1_ref[:, cs], preferred_element_type=F32)
        a = jnp.square(jnp.maximum(a, 0.0)).astype(BF16)
        part = jnp.dot(a, w2_ref[cs, :], preferred_element_type=F32)
        acc = part if acc is None else acc + part
    o_ref[...] = x + _rms(acc, gpost_ref[...])


def ffn(x2, w1, w2, g_pre, g_post, *, tm=512, ff_chunk=1024):
    T, D = x2.shape
    FF = w1.shape[1]
    tm = min(tm, T)
    return pl.pallas_call(
        functools.partial(_ffn_kernel, ff_chunk=ff_chunk),
        grid=(T // tm,),
        in_specs=[
            pl.BlockSpec((tm, D), lambda i: (i, 0)),
            pl.BlockSpec((D, FF), lambda i: (0, 0)),
            pl.BlockSpec((FF, D), lambda i: (0, 0)),
            pl.BlockSpec((1, D), lambda i: (0, 0)),
            pl.BlockSpec((1, D), lambda i: (0, 0)),
        ],
        out_specs=pl.BlockSpec((tm, D), lambda i: (i, 0)),
        out_shape=jax.ShapeDtypeStruct((T, D), F32),
        compiler_params=_params(("parallel",), 56),
        name="ffn",
    )(x2, w1, w2, g_pre, g_post)


def _regroup_in_proj(w_in, b_in):
    def cols(names):
        return [slice(_IN_OFFSETS[n][0], _IN_OFFSETS[n][0] + _IN_OFFSETS[n][1]) for n in names]
    w_wide = jnp.concatenate([w_in[:, s] for s in cols(_WIDE_ORDER)], axis=1).astype(BF16)
    b_wide = jnp.concatenate([b_in[s] for s in cols(_WIDE_ORDER)])[None, :]
    pad = LANES - N_GATE_LANES
    w_narrow = jnp.pad(jnp.concatenate([w_in[:, s] for s in cols(_NARROW_ORDER)], axis=1),
                       ((0, 0), (0, pad))).astype(BF16)
    b_narrow = jnp.pad(jnp.concatenate([b_in[s] for s in cols(_NARROW_ORDER)]), (0, pad))[None, :]
    return w_wide, b_wide, w_narrow, b_narrow


def _time_on_lanes(g3, blk):
    B, S, _ = g3.shape
    return g3[:, :, :N_GATE_LANES].reshape(B, S // blk, blk, N_GATE_LANES).transpose(0, 1, 3, 2)


def _layer(x3, mem2, norms, w_in, b_in, conv_w, mlstm_g, gmlp_g, gmlp_ws, gmlp_bs,
           w_branch, w_out, w_xq, w_xkv, w_xo, w_ff1, w_ff2, *, fox_blk, ml_chunk):
    B, S, D = x3.shape
    T = B * S
    M = mem2.shape[0] // B
    g = lambda idx: norms[idx][None, :]
    x2 = x3.reshape(T, D)

    w_wide, b_wide, w_narrow, b_narrow = _regroup_in_proj(w_in, b_in)
    z2, zn2 = in_proj(x2, g(0), w_wide, b_wide, w_narrow, b_narrow)
    z3 = z2.reshape(B, S, N_WIDE)
    g3 = gates(zn2.reshape(B, S, LANES))

    y_fox = fox_attention(z3, g3, blk=fox_blk)
    y_ml = mlstm(z3, conv_w, mlstm_g[None, :], g3, _time_on_lanes(g3, ml_chunk), L=ml_chunk)
    y_g = gmlp(z3, gmlp_g[None, :], gmlp_ws, gmlp_bs.T)
    x2 = merge(x2, z2, y_fox.reshape(T, FOX_WIDTH), y_ml.reshape(T, ML_WIDTH), y_g.reshape(T, G_WIDTH),
               w_branch.astype(BF16), w_out.astype(BF16), g(1))

    kv = norm_matmul(mem2, g(4), w_xkv.astype(BF16))
    x3 = xattn(x2.reshape(B, S, D), kv.reshape(B, M, 2 * D), w_xq.astype(BF16), w_xo.astype(BF16), g(2), g(3))

    x2 = ffn(x3.reshape(T, D), w_ff1.astype(BF16), w_ff2.astype(BF16), g(5), g(6))
    return x2.reshape(B, S, D)


def kernel(x, mem, norms, w_in, b_in, conv_w, mlstm_norm, gmlp_norm, gmlp_ws, gmlp_bs, w_branch, w_out, w_xq, w_xkv, w_xo, w_ff1, w_ff2):
    B, M, D = mem.shape
    S = x.shape[1]
    mem2 = mem.reshape(B * M, D)
    fox_blk = min(256, S)
    for l in range(norms.shape[0]):
        x = _layer(x, mem2, norms[l], w_in[l], b_in[l], conv_w[l], mlstm_norm[l], gmlp_norm[l],
                   gmlp_ws[l], gmlp_bs[l], w_branch[l], w_out[l], w_xq[l], w_xkv[l], w_xo[l],
                   w_ff1[l], w_ff2[l], fox_blk=fox_blk, ml_chunk=min(ML_CHUNK, S))
    return x
```

```python
import functools

import jax
import jax.numpy as jnp
import numpy as np
from jax import lax
from jax.experimental import pallas as pl
from jax.experimental.pallas import tpu as pltpu

F32 = jnp.float32
BF16 = jnp.bfloat16

EPS = 1e-6
D_MODEL = 1024
CHUNK = 64
FOX_HEADS = 8
FOX_HEAD_DIM = 64
FOX_WIDTH = FOX_HEADS * FOX_HEAD_DIM
ML_HEADS = 4
ML_HEAD_DIM = 128
ML_WIDTH = ML_HEADS * ML_HEAD_DIM
ML_CHUNK = 128
CONV_WIDTH = 4
G_GROUPS = 4
G_GROUP_DIM = 128
G_WIDTH = G_GROUPS * G_GROUP_DIM
G_SPAN = 128
N_BRANCH = 3
X_HEADS = 4
X_HEAD_DIM = D_MODEL // X_HEADS
D_FF = 4 * D_MODEL

LANES = 128
MIB = 1024 * 1024

_IN_SPLITS = (
    ("fox_q", FOX_WIDTH), ("fox_k", FOX_WIDTH), ("fox_v", FOX_WIDTH), ("fox_f", FOX_HEADS),
    ("ml_q", ML_WIDTH), ("ml_k", ML_WIDTH), ("ml_v", ML_WIDTH),
    ("ml_i", ML_HEADS), ("ml_f", ML_HEADS), ("ml_o", ML_WIDTH),
    ("g_u", G_WIDTH), ("g_v", G_WIDTH),
    ("gate", N_BRANCH * D_MODEL),
)
_IN_OFFSETS = {}
_off = 0
for _name, _size in _IN_SPLITS:
    _IN_OFFSETS[_name] = (_off, _size)
    _off += _size

_WIDE_ORDER = ("gate", "fox_q", "fox_k", "fox_v", "ml_q", "ml_k", "ml_v", "ml_o", "g_u", "g_v")
_WIDE_OFFSETS = {}
_off = 0
for _name in _WIDE_ORDER:
    _WIDE_OFFSETS[_name] = _off
    _off += _IN_OFFSETS[_name][1]
N_WIDE = _off
_NARROW_ORDER = ("fox_f", "ml_i", "ml_f")
LANE_FOX_F = 0
LANE_ML_I = FOX_HEADS
LANE_ML_F = FOX_HEADS + ML_HEADS
N_GATE_LANES = FOX_HEADS + 2 * ML_HEADS


def _rms(x, g):
    return x * lax.rsqrt(jnp.mean(x * x, axis=-1, keepdims=True) + EPS) * g


def _log_sigmoid(x):
    return jnp.minimum(x, 0.0) - jnp.log1p(jnp.exp(-jnp.abs(x)))


def _params(semantics, vmem_mib):
    return pltpu.CompilerParams(dimension_semantics=semantics, vmem_limit_bytes=vmem_mib * MIB)


def _in_proj_kernel(x_ref, g_ref, w_ref, b_ref, wn_ref, bn_ref, z_ref, zn_ref, h_ref):
    @pl.when(pl.program_id(1) == 0)
    def _():
        h = _rms(x_ref[...], g_ref[...]).astype(BF16)
        h_ref[...] = h
        zn_ref[...] = jnp.dot(h, wn_ref[...], preferred_element_type=F32) + bn_ref[...]

    z = jnp.dot(h_ref[...], w_ref[...], preferred_element_type=F32) + b_ref[...]
    z_ref[...] = z.astype(z_ref.dtype)


def in_proj(x2, g, w_wide, b_wide, w_narrow, b_narrow, *, tm=1024, tn=1536):
    T, D = x2.shape
    N = w_wide.shape[1]
    tm = min(tm, T)
    return pl.pallas_call(
        _in_proj_kernel,
        grid=(T // tm, N // tn),
        in_specs=[
            pl.BlockSpec((tm, D), lambda i, j: (i, 0)),
            pl.BlockSpec((1, D), lambda i, j: (0, 0)),
            pl.BlockSpec((D, tn), lambda i, j: (0, j)),
            pl.BlockSpec((1, tn), lambda i, j: (0, j)),
            pl.BlockSpec((D, LANES), lambda i, j: (0, 0)),
            pl.BlockSpec((1, LANES), lambda i, j: (0, 0)),
        ],
        out_specs=[
            pl.BlockSpec((tm, tn), lambda i, j: (i, j)),
            pl.BlockSpec((tm, LANES), lambda i, j: (i, 0)),
        ],
        out_shape=[
            jax.ShapeDtypeStruct((T, N), BF16),
            jax.ShapeDtypeStruct((T, LANES), F32),
        ],
        scratch_shapes=[pltpu.VMEM((tm, D), BF16)],
        compiler_params=_params(("parallel", "arbitrary"), 40),
        name="in_proj",
    )(x2, g, w_wide, b_wide, w_narrow, b_narrow)


def _gates_kernel(zn_ref, g_ref, *, rows):
    S = zn_ref.shape[0]
    lane = lax.broadcasted_iota(jnp.int32, (rows, LANES), 1)
    keep_raw = jnp.logical_and(lane >= LANE_ML_I, lane < LANE_ML_F)
    r = lax.broadcasted_iota(jnp.int32, (rows, rows), 0)
    c = lax.broadcasted_iota(jnp.int32, (rows, rows), 1)
    tril = jnp.where(r >= c, 1.0, 0.0).astype(F32)

    def body(i, carry):
        sl = pl.ds(pl.multiple_of(i * rows, rows), rows)
        z = zn_ref[sl, :]
        ls = _log_sigmoid(z)
        cs = jnp.dot(tril, ls, precision=lax.Precision.HIGHEST, preferred_element_type=F32) + carry
        g_ref[sl, :] = jnp.where(keep_raw, z, cs)
        return cs[rows - 1:rows, :]

    lax.fori_loop(0, S // rows, body, jnp.zeros((1, LANES), F32))


def gates(zn3, *, rows=256):
    B, S, _ = zn3.shape
    rows = min(rows, S)
    return pl.pallas_call(
        functools.partial(_gates_kernel, rows=rows),
        grid=(B,),
        in_specs=[pl.BlockSpec((None, S, LANES), lambda b: (b, 0, 0))],
        out_specs=pl.BlockSpec((None, S, LANES), lambda b: (b, 0, 0)),
        out_shape=jax.ShapeDtypeStruct((B, S, LANES), F32),
        compiler_params=_params(("parallel",), 16),
        name="gates",
    )(zn3)


FOX_AUG = LANES
FOX_PAIR = 2 * FOX_HEAD_DIM
FOX_VT_ROWS = FOX_HEAD_DIM + 16


def _fox_place_matrices():
    pk = np.zeros((3 * LANES, FOX_HEADS * FOX_AUG), np.float32)
    pq = np.zeros((3 * LANES, FOX_HEADS * FOX_AUG), np.float32)
    ck = np.zeros((1, FOX_HEADS * FOX_AUG), np.float32)
    cq = np.zeros((1, FOX_HEADS * FOX_AUG), np.float32)
    for h in range(FOX_HEADS):
        for piece in range(3):
            pk[piece * LANES + LANE_FOX_F + h, h * FOX_AUG + piece] = -1.0
            pq[piece * LANES + LANE_FOX_F + h, h * FOX_AUG + 3 + piece] = 1.0
            ck[0, h * FOX_AUG + 3 + piece] = 1.0
            cq[0, h * FOX_AUG + piece] = 1.0
    return (jnp.asarray(pk, BF16), jnp.asarray(pq, BF16), jnp.asarray(ck), jnp.asarray(cq))


def _fox_bias_lanes(f, place_ref, ones_ref):
    hi = f.astype(BF16)
    r1 = f - hi.astype(F32)
    mid = r1.astype(BF16)
    lo = (r1 - mid.astype(F32)).astype(BF16)
    x = jnp.concatenate([hi, mid, lo], axis=-1)
    return (jnp.dot(x, place_ref[...], preferred_element_type=F32) + ones_ref[...]).astype(BF16)


def _fox_kernel(q_ref, k_ref, v_ref, g_ref, pk_ref, pq_ref, ck_ref, cq_ref, o_ref,
                kaug_ref, vt_ref, qaug_ref, m_ref, acc_ref, al_ref, st_ref, p_ref, *, blk):
    S = k_ref.shape[0]
    qi = pl.program_id(1)
    key_pos = lax.broadcasted_iota(jnp.int32, (blk, blk), 0)
    qry_pos = lax.broadcasted_iota(jnp.int32, (blk, blk), 1)
    causal = key_pos <= qry_pos

    @pl.when(qi == 0)
    def _():
        ones = jnp.ones((FOX_VT_ROWS - FOX_HEAD_DIM, blk), BF16)
        for c in range(S // blk):
            rows = slice(c * blk, (c + 1) * blk)
            kaug_ref[rows, :] = _fox_bias_lanes(g_ref[rows, :], pk_ref, ck_ref)
            for j in range(FOX_HEADS // 2):
                vt = v_ref[rows, j * FOX_PAIR:(j + 1) * FOX_PAIR].T
                for hh in range(2):
                    vt_ref[c, 2 * j + hh, :FOX_HEAD_DIM, :] = vt[hh * FOX_HEAD_DIM:(hh + 1) * FOX_HEAD_DIM, :]
                    vt_ref[c, 2 * j + hh, FOX_HEAD_DIM:, :] = ones

    q_rows = pl.ds(pl.multiple_of(qi * blk, blk), blk)
    qaug = _fox_bias_lanes(g_ref[q_rows, :], pq_ref, cq_ref)
    half = lax.broadcasted_iota(jnp.int32, (blk, FOX_PAIR), 1) // FOX_HEAD_DIM
    scale = jnp.asarray(FOX_HEAD_DIM ** -0.5, BF16)
    for h in range(FOX_HEADS):
        pair = slice((h // 2) * FOX_PAIR, (h // 2 + 1) * FOX_PAIR)
        qm = jnp.where(half == h % 2, q_ref[:, pair] * scale, jnp.zeros((), BF16))
        qaug_ref[h] = jnp.concatenate([qm, qaug[:, h * FOX_AUG:(h + 1) * FOX_AUG]], axis=-1).T
    m_ref[...] = jnp.full_like(m_ref, -jnp.inf)
    acc_ref[...] = jnp.zeros_like(acc_ref)

    def score_head(kb, slot, h):
        ks = pl.ds(pl.multiple_of(kb * blk, blk), blk)
        pair = slice((h // 2) * FOX_PAIR, (h // 2 + 1) * FOX_PAIR)
        kk = jnp.concatenate([k_ref[ks, pair], kaug_ref[ks, h * FOX_AUG:(h + 1) * FOX_AUG]], axis=-1)
        st_ref[slot, h] = jnp.dot(kk, qaug_ref[h], preferred_element_type=F32)

    def softmax_head(slot, h, masked):
        st = st_ref[slot, h]
        if masked:
            st = jnp.where(causal, st, -jnp.inf)
        m_old = m_ref[h:h + 1, :]
        m_new = jnp.maximum(m_old, jnp.max(st, axis=0, keepdims=True))
        al_ref[h:h + 1, :] = jnp.exp(m_old - m_new)
        m_ref[h:h + 1, :] = m_new
        p_ref[h] = jnp.exp(st - m_new).astype(BF16)

    def value_head(kb, h):
        pv = jnp.dot(vt_ref[kb, h], p_ref[h], preferred_element_type=F32)
        acc_ref[h] = al_ref[h:h + 1, :] * acc_ref[h] + pv

    def advance(kb, slot, masked, prefetch):
        for h in range(FOX_HEADS):
            if prefetch:
                score_head(kb + 1, 1 - slot, h)
            softmax_head(slot, h, masked)
            if h > 0:
                value_head(kb, h - 1)
        value_head(kb, FOX_HEADS - 1)

    for h in range(FOX_HEADS):
        score_head(0, 0, h)

    def two_blocks(i, carry):
        advance(2 * i, 0, False, True)
        advance(2 * i + 1, 1, False, True)
        return carry

    lax.fori_loop(0, qi // 2, two_blocks, 0)

    @pl.when(qi % 2 == 0)
    def _():
        advance(qi, 0, True, False)

    @pl.when(qi % 2 == 1)
    def _():
        advance(qi - 1, 0, False, True)
        advance(qi, 1, True, False)

    outs = []
    for j in range(FOX_HEADS // 2):
        tops = []
        for h in (2 * j, 2 * j + 1):
            a = acc_ref[h]
            tops.append(a[:FOX_HEAD_DIM, :] / a[FOX_HEAD_DIM:FOX_HEAD_DIM + 1, :])
        outs.append(jnp.concatenate(tops, axis=0).T)
    o_ref[...] = jnp.concatenate(outs, axis=-1).astype(o_ref.dtype)


def fox_attention(z3, g3, *, blk=256):
    B, S, _ = z3.shape
    wq = _WIDE_OFFSETS["fox_q"] // FOX_WIDTH
    wk = _WIDE_OFFSETS["fox_k"] // FOX_WIDTH
    wv = _WIDE_OFFSETS["fox_v"] // FOX_WIDTH
    pk, pq, ck, cq = _fox_place_matrices()
    const = lambda a: pl.BlockSpec(a.shape, lambda b, i: (0,) * a.ndim)
    return pl.pallas_call(
        functools.partial(_fox_kernel, blk=blk),
        grid=(B, S // blk),
        in_specs=[
            pl.BlockSpec((None, blk, FOX_WIDTH), lambda b, i: (b, i, wq)),
            pl.BlockSpec((None, S, FOX_WIDTH), lambda b, i: (b, 0, wk)),
            pl.BlockSpec((None, S, FOX_WIDTH), lambda b, i: (b, 0, wv)),
            pl.BlockSpec((None, S, LANES), lambda b, i: (b, 0, 0)),
            const(pk), const(pq), const(ck), const(cq),
        ],
        out_specs=pl.BlockSpec((None, blk, FOX_WIDTH), lambda b, i: (b, i, 0)),
        out_shape=jax.ShapeDtypeStruct((B, S, FOX_WIDTH), BF16),
        scratch_shapes=[
            pltpu.VMEM((S, FOX_HEADS * FOX_AUG), BF16),
            pltpu.VMEM((S // blk, FOX_HEADS, FOX_VT_ROWS, blk), BF16),
            pltpu.VMEM((FOX_HEADS, FOX_PAIR + FOX_AUG, blk), BF16),
            pltpu.VMEM((FOX_HEADS, blk), F32),
            pltpu.VMEM((FOX_HEADS, FOX_VT_ROWS, blk), F32),
            pltpu.VMEM((FOX_HEADS, blk), F32),
            pltpu.VMEM((2, FOX_HEADS, blk, blk), F32),
            pltpu.VMEM((FOX_HEADS, blk, blk), BF16),
        ],
        compiler_params=_params(("parallel", "arbitrary"), 48),
        name="fox",
    )(z3, z3, z3, g3, pk, pq, ck, cq)


def _mlstm_kernel(q_ref, k_ref, v_ref, o_ref, cw_ref, gn_ref, g_ref, gt_ref, y_ref,
                  cn_ref, st_ref, halo_ref, *, L):
    S = q_ref.shape[0]
    HALO = 8
    r = lax.broadcasted_iota(jnp.int32, (L, L), 0)
    c = lax.broadcasted_iota(jnp.int32, (L, L), 1)
    tril = r >= c
    lane = lax.broadcasted_iota(jnp.int32, (L, LANES), 1)
    ones_col = jnp.where(lane == 0, 1.0, 0.0).astype(BF16)
    k_scale = ML_HEAD_DIM ** -0.5
    dn_t = (((1,), (1,)), ((), ()))

    cn_ref[...] = jnp.zeros_like(cn_ref)
    st_ref[...] = jnp.zeros_like(st_ref)
    halo_ref[...] = jnp.zeros_like(halo_ref)

    def conv_silu(x_chunk, halo, w):
        xx = jnp.concatenate([halo, x_chunk], axis=0)
        y = w[CONV_WIDTH - 1:CONV_WIDTH, :] * x_chunk
        for j in range(CONV_WIDTH - 1):
            sh = CONV_WIDTH - 1 - j
            y = y + w[j:j + 1, :] * xx[HALO - sh:HALO - sh + L, :]
        return y * jax.nn.sigmoid(y)

    def chunk(ci, _):
        rows = pl.ds(pl.multiple_of(ci * L, L), L)
        xq = q_ref[rows, :].astype(F32)
        xk = k_ref[rows, :].astype(F32)
        qa = conv_silu(xq, halo_ref[:, :ML_WIDTH], cw_ref[:, :ML_WIDTH])
        ka = conv_silu(xk, halo_ref[:, ML_WIDTH:], cw_ref[:, ML_WIDTH:]) * k_scale
        halo_ref[:, :ML_WIDTH] = xq[L - HALO:, :]
        halo_ref[:, ML_WIDTH:] = xk[L - HALO:, :]
        gcol = g_ref[rows, :]
        grow = gt_ref[ci]

        for h in range(ML_HEADS):
            cols = slice(h * ML_HEAD_DIM, (h + 1) * ML_HEAD_DIM)
            qb = qa[:, cols].astype(BF16)
            kf = ka[:, cols]
            vb = v_ref[rows, cols]
            f_c = gcol[:, LANE_ML_F + h:LANE_ML_F + h + 1]
            i_c = gcol[:, LANE_ML_I + h:LANE_ML_I + h + 1]
            f_r = grow[LANE_ML_F + h:LANE_ML_F + h + 1, :]
            i_r = grow[LANE_ML_I + h:LANE_ML_I + h + 1, :]
            f_prev = st_ref[h, 0:1, 0:1]
            m_prev = st_ref[h, 0:1, 1:2]

            d = jnp.where(tril, f_c - f_r + i_r, -jnp.inf)
            inter = f_c - f_prev + m_prev
            m = jnp.maximum(inter, jnp.max(d, axis=-1, keepdims=True))
            w_inter = jnp.exp(inter - m)
            p = lax.dot_general(qb, kf.astype(BF16), dn_t, preferred_element_type=F32) * jnp.exp(d - m)
            inter_out = jnp.dot(qb, cn_ref[h].astype(BF16), preferred_element_type=F32)
            num = w_inter * inter_out[:, :ML_HEAD_DIM] + jnp.dot(p.astype(BF16), vb, preferred_element_type=F32)
            den = w_inter * inter_out[:, ML_HEAD_DIM:ML_HEAD_DIM + 1] + jnp.sum(p, axis=-1, keepdims=True)
            hh = num / jnp.maximum(jnp.abs(den), jnp.exp(-m))

            m_new = m[L - 1:L, :]
            f_end = f_c[L - 1:L, :]
            decay = jnp.exp(f_end - f_prev + m_prev - m_new)
            w_s = jnp.exp(f_end - f_c + i_c - m_new)
            kw = (w_s * kf).astype(BF16)
            v_aug = jnp.concatenate([vb, ones_col], axis=-1)
            upd = lax.dot_general(kw, v_aug, (((0,), (0,)), ((), ())), preferred_element_type=F32)
            cn_ref[h] = decay * cn_ref[h] + upd
            st_ref[h, 0:1, 0:1] = f_end
            st_ref[h, 0:1, 1:2] = m_new

            hn = hh * lax.rsqrt(jnp.mean(hh * hh, axis=-1, keepdims=True) + EPS) * gn_ref[:, cols]
            y = jax.nn.sigmoid(o_ref[rows, cols].astype(F32)) * hn
            y_ref[rows, cols] = y.astype(y_ref.dtype)
        return 0

    lax.fori_loop(0, S // L, chunk, 0)


def mlstm(z3, conv_w, mlstm_g, g3, gt4, *, L=CHUNK):
    B, S, _ = z3.shape
    blocks = [_WIDE_OFFSETS[n] // ML_WIDTH for n in ("ml_q", "ml_k", "ml_v", "ml_o")]
    seq_spec = lambda idx: pl.BlockSpec((None, S, ML_WIDTH), lambda b: (b, 0, idx))
    return pl.pallas_call(
        functools.partial(_mlstm_kernel, L=L),
        grid=(B,),
        in_specs=[
            seq_spec(blocks[0]), seq_spec(blocks[1]), seq_spec(blocks[2]), seq_spec(blocks[3]),
            pl.BlockSpec((CONV_WIDTH, 2 * ML_WIDTH), lambda b: (0, 0)),
            pl.BlockSpec((1, ML_WIDTH), lambda b: (0, 0)),
            pl.BlockSpec((None, S, LANES), lambda b: (b, 0, 0)),
            pl.BlockSpec((None, S // L, N_GATE_LANES, L), lambda b: (b, 0, 0, 0)),
        ],
        out_specs=pl.BlockSpec((None, S, ML_WIDTH), lambda b: (b, 0, 0)),
        out_shape=jax.ShapeDtypeStruct((B, S, ML_WIDTH), BF16),
        scratch_shapes=[
            pltpu.VMEM((ML_HEADS, ML_HEAD_DIM, 2 * ML_HEAD_DIM), F32),
            pltpu.VMEM((ML_HEADS, 8, LANES), F32),
            pltpu.VMEM((8, 2 * ML_WIDTH), F32),
        ],
        compiler_params=_params(("parallel",), 40),
        name="mlstm",
    )(z3, z3, z3, z3, conv_w, mlstm_g, g3, gt4)


def _gmlp_kernel(u_ref, v_ref, gn_ref, ws_ref, bst_ref, y_ref):
    rows = u_ref.shape[0]
    u = jax.nn.gelu(u_ref[...].astype(F32))
    v = jax.nn.gelu(v_ref[...].astype(F32))
    mu = jnp.mean(v, axis=-1, keepdims=True)
    vc = v - mu
    vn = (vc * lax.rsqrt(jnp.mean(vc * vc, axis=-1, keepdims=True) + EPS) * gn_ref[...]).astype(BF16)
    r = lax.broadcasted_iota(jnp.int32, (G_SPAN, G_SPAN), 0) // CHUNK
    c = lax.broadcasted_iota(jnp.int32, (G_SPAN, G_SPAN), 1) // CHUNK
    mask = r >= c
    for g in range(G_GROUPS):
        cols = slice(g * G_GROUP_DIM, (g + 1) * G_GROUP_DIM)
        w = jnp.where(mask, ws_ref[g], 0.0).astype(BF16)
        bias = bst_ref[:, g:g + 1]
        for s in range(rows // G_SPAN):
            rs = slice(s * G_SPAN, (s + 1) * G_SPAN)
            mixed = jnp.dot(w, vn[rs, cols], preferred_element_type=F32) + bias
            y_ref[rs, cols] = (u[rs, cols] * mixed).astype(y_ref.dtype)


def gmlp(z3, gmlp_g, ws, bs_t, *, rows=512):
    B, S, _ = z3.shape
    rows = min(rows, S)
    bu = _WIDE_OFFSETS["g_u"] // G_WIDTH
    bv = _WIDE_OFFSETS["g_v"] // G_WIDTH
    return pl.pallas_call(
        _gmlp_kernel,
        grid=(B, S // rows),
        in_specs=[
            pl.BlockSpec((None, rows, G_WIDTH), lambda b, i: (b, i, bu)),
            pl.BlockSpec((None, rows, G_WIDTH), lambda b, i: (b, i, bv)),
            pl.BlockSpec((1, G_WIDTH), lambda b, i: (0, 0)),
            pl.BlockSpec((G_GROUPS, G_SPAN, G_SPAN), lambda b, i: (0, 0, 0)),
            pl.BlockSpec((G_SPAN, G_GROUPS), lambda b, i: (0, 0)),
        ],
        out_specs=pl.BlockSpec((None, rows, G_WIDTH), lambda b, i: (b, i, 0)),
        out_shape=jax.ShapeDtypeStruct((B, S, G_WIDTH), BF16),
        compiler_params=_params(("parallel", "parallel"), 16),
        name="gmlp",
    )(z3, z3, gmlp_g, ws, bs_t)


def _merge_kernel(x_ref, gate_ref, ya_ref, yb_ref, yc_ref, wb_ref, wo_ref, gpost_ref, o_ref):
    merged = None
    for n, y_ref in enumerate((ya_ref, yb_ref, yc_ref)):
        br = jnp.dot(y_ref[...], wb_ref[n], preferred_element_type=F32)
        gt = jax.nn.sigmoid(gate_ref[:, n * D_MODEL:(n + 1) * D_MODEL].astype(F32))
        merged = gt * br if merged is None else merged + gt * br
    y = jnp.dot(merged.astype(BF16), wo_ref[...], preferred_element_type=F32)
    o_ref[...] = x_ref[...] + _rms(y, gpost_ref[...])


def merge(x2, z2, y_fox, y_ml, y_g, w_branch, w_out, g_post, *, tm=512):
    T, D = x2.shape
    tm = min(tm, T)
    row = lambda w: pl.BlockSpec((tm, w), lambda i: (i, 0))
    return pl.pallas_call(
        _merge_kernel,
        grid=(T // tm,),
        in_specs=[
            row(D),
            row(N_BRANCH * D),
            row(FOX_WIDTH), row(ML_WIDTH), row(G_WIDTH),
            pl.BlockSpec((N_BRANCH, FOX_WIDTH, D), lambda i: (0, 0, 0)),
            pl.BlockSpec((D, D), lambda i: (0, 0)),
            pl.BlockSpec((1, D), lambda i: (0, 0)),
        ],
        out_specs=row(D),
        out_shape=jax.ShapeDtypeStruct((T, D), F32),
        compiler_params=_params(("parallel",), 48),
        name="merge",
    )(x2, z2, y_fox, y_ml, y_g, w_branch, w_out, g_post)


def _norm_matmul_kernel(x_ref, g_ref, w_ref, o_ref):
    h = _rms(x_ref[...], g_ref[...]).astype(BF16)
    o_ref[...] = jnp.dot(h, w_ref[...], preferred_element_type=F32).astype(o_ref.dtype)


def norm_matmul(x2, g, w, *, tm=512):
    T, D = x2.shape
    N = w.shape[1]
    tm = min(tm, T)
    return pl.pallas_call(
        _norm_matmul_kernel,
        grid=(T // tm,),
        in_specs=[
            pl.BlockSpec((tm, D), lambda i: (i, 0)),
            pl.BlockSpec((1, D), lambda i: (0, 0)),
            pl.BlockSpec((D, N), lambda i: (0, 0)),
        ],
        out_specs=pl.BlockSpec((tm, N), lambda i: (i, 0)),
        out_shape=jax.ShapeDtypeStruct((T, N), BF16),
        compiler_params=_params(("parallel",), 40),
        name="mem_kv",
    )(x2, g, w)


def _xattn_kernel(x_ref, kv_ref, wq_ref, wo_ref, gpre_ref, gpost_ref, o_ref):
    x = x_ref[...]
    h = _rms(x, gpre_ref[...]).astype(BF16)
    q = jnp.dot(h, wq_ref[...], preferred_element_type=F32).astype(BF16)
    dn = (((1,), (1,)), ((), ()))
    outs = []
    for hd in range(X_HEADS):
        cols = slice(hd * X_HEAD_DIM, (hd + 1) * X_HEAD_DIM)
        k = kv_ref[:, cols]
        v = kv_ref[:, D_MODEL + hd * X_HEAD_DIM:D_MODEL + (hd + 1) * X_HEAD_DIM]
        s = lax.dot_general(q[:, cols], k, dn, preferred_element_type=F32) * (X_HEAD_DIM ** -0.5)
        s = s - jnp.max(s, axis=-1, keepdims=True)
        e = jnp.exp(s)
        p = e / jnp.sum(e, axis=-1, keepdims=True)
        outs.append(jnp.dot(p.astype(BF16), v, preferred_element_type=F32).astype(BF16))
    o = jnp.concatenate(outs, axis=-1)
    y = jnp.dot(o, wo_ref[...], preferred_element_type=F32)
    o_ref[...] = x + _rms(y, gpost_ref[...])


def xattn(x3, kv3, w_q, w_o, g_pre, g_post, *, tm=512):
    B, S, D = x3.shape
    M = kv3.shape[1]
    tm = min(tm, S)
    const = lambda shape: pl.BlockSpec(shape, lambda b, i: (0,) * len(shape))
    return pl.pallas_call(
        _xattn_kernel,
        grid=(B, S // tm),
        in_specs=[
            pl.BlockSpec((None, tm, D), lambda b, i: (b, i, 0)),
            pl.BlockSpec((None, M, 2 * D), lambda b, i: (b, 0, 0)),
            const((D, D)), const((D, D)), const((1, D)), const((1, D)),
        ],
        out_specs=pl.BlockSpec((None, tm, D), lambda b, i: (b, i, 0)),
        out_shape=jax.ShapeDtypeStruct((B, S, D), F32),
        compiler_params=_params(("parallel", "arbitrary"), 48),
        name="xattn",
    )(x3, kv3, w_q, w_o, g_pre, g_post)


def _ffn_kernel(x_ref, w1_ref, w2_ref, gpre_ref, gpost_ref, o_ref, *, ff_chunk):
    x = x_ref[...]
    h = _rms(x, gpre_ref[...]).astype(BF16)
    acc = None
    for c in range(w1_ref.shape[1] // ff_chunk):
        cs = slice(c * ff_chunk, (c + 1) * ff_chunk)
        a = jnp.dot(h, w1_ref[:, cs], preferred_element_type=F32)
        a = jnp.square(jnp.maximum(a, 0.0)).astype(BF16)
        part = jnp.dot(a, w2_ref[cs, :], preferred_element_type=F32)
        acc = part if acc is None else acc + part
    o_ref[...] = x + _rms(acc, gpost_ref[...])


def ffn(x2, w1, w2, g_pre, g_post, *, tm=512, ff_chunk=1024):
    T, D = x2.shape
    FF = w1.shape[1]
    tm = min(tm, T)
    return pl.pallas_call(
        functools.partial(_ffn_kernel, ff_chunk=ff_chunk),
        grid=(T // tm,),
        in_specs=[
            pl.BlockSpec((tm, D), lambda i: (i, 0)),
            pl.BlockSpec((D, FF), lambda i: (0, 0)),
            pl.BlockSpec((FF, D), lambda i: (0, 0)),
            pl.BlockSpec((1, D), lambda i: (0, 0)),
            pl.BlockSpec((1, D), lambda i: (0, 0)),
        ],
        out_specs=pl.BlockSpec((tm, D), lambda i: (i, 0)),
        out_shape=jax.ShapeDtypeStruct((T, D), F32),
        compiler_params=_params(("parallel",), 56),
        name="ffn",
    )(x2, w1, w2, g_pre, g_post)


def _regroup_in_proj(w_in, b_in):
    def cols(names):
        return [slice(_IN_OFFSETS[n][0], _IN_OFFSETS[n][0] + _IN_OFFSETS[n][1]) for n in names]
    w_wide = jnp.concatenate([w_in[:, s] for s in cols(_WIDE_ORDER)], axis=1).astype(BF16)
    b_wide = jnp.concatenate([b_in[s] for s in cols(_WIDE_ORDER)])[None, :]
    pad = LANES - N_GATE_LANES
    w_narrow = jnp.pad(jnp.concatenate([w_in[:, s] for s in cols(_NARROW_ORDER)], axis=1),
                       ((0, 0), (0, pad))).astype(BF16)
    b_narrow = jnp.pad(jnp.concatenate([b_in[s] for s in cols(_NARROW_ORDER)]), (0, pad))[None, :]
    return w_wide, b_wide, w_narrow, b_narrow


def _time_on_lanes(g3, blk):
    B, S, _ = g3.shape
    return g3[:, :, :N_GATE_LANES].reshape(B, S // blk, blk, N_GATE_LANES).transpose(0, 1, 3, 2)


def _layer(x3, mem2, norms, w_in, b_in, conv_w, mlstm_g, gmlp_g, gmlp_ws, gmlp_bs,
           w_branch, w_out, w_xq, w_xkv, w_xo, w_ff1, w_ff2, *, fox_blk, ml_chunk):
    B, S, D = x3.shape
    T = B * S
    M = mem2.shape[0] // B
    g = lambda idx: norms[idx][None, :]
    x2 = x3.reshape(T, D)

    w_wide, b_wide, w_narrow, b_narrow = _regroup_in_proj(w_in, b_in)
    z2, zn2 = in_proj(x2, g(0), w_wide, b_wide, w_narrow, b_narrow)
    z3 = z2.reshape(B, S, N_WIDE)
    g3 = gates(zn2.reshape(B, S, LANES))

    y_fox = fox_attention(z3, g3, blk=fox_blk)
    y_ml = mlstm(z3, conv_w, mlstm_g[None, :], g3, _time_on_lanes(g3, ml_chunk), L=ml_chunk)
    y_g = gmlp(z3, gmlp_g[None, :], gmlp_ws, gmlp_bs.T)
    x2 = merge(x2, z2, y_fox.reshape(T, FOX_WIDTH), y_ml.reshape(T, ML_WIDTH), y_g.reshape(T, G_WIDTH),
               w_branch.astype(BF16), w_out.astype(BF16), g(1))

    kv = norm_matmul(mem2, g(4), w_xkv.astype(BF16))
    x3 = xattn(x2.reshape(B, S, D), kv.reshape(B, M, 2 * D), w_xq.astype(BF16), w_xo.astype(BF16), g(2), g(3))

    x2 = ffn(x3.reshape(T, D), w_ff1.astype(BF16), w_ff2.astype(BF16), g(5), g(6))
    return x2.reshape(B, S, D)


def kernel(x, mem, norms, w_in, b_in, conv_w, mlstm_norm, gmlp_norm, gmlp_ws, gmlp_bs, w_branch, w_out, w_xq, w_xkv, w_xo, w_ff1, w_ff2):
    B, M, D = mem.shape
    S = x.shape[1]
    mem2 = mem.reshape(B * M, D)
    fox_blk = min(256, S)
    for l in range(norms.shape[0]):
        x = _layer(x, mem2, norms[l], w_in[l], b_in[l], conv_w[l], mlstm_norm[l], gmlp_norm[l],
                   gmlp_ws[l], gmlp_bs[l], w_branch[l], w_out[l], w_xq[l], w_xkv[l], w_xo[l],
                   w_ff1[l], w_ff2[l], fox_blk=fox_blk, ml_chunk=min(ML_CHUNK, S))
    return x
```

```python
import functools

import jax
import jax.numpy as jnp
import numpy as np
from jax import lax
from jax.experimental import pallas as pl
from jax.experimental.pallas import tpu as pltpu

F32 = jnp.float32
BF16 = jnp.bfloat16

EPS = 1e-6
D_MODEL = 1024
CHUNK = 64
FOX_HEADS = 8
FOX_HEAD_DIM = 64
FOX_WIDTH = FOX_HEADS * FOX_HEAD_DIM
ML_HEADS = 4
ML_HEAD_DIM = 128
ML_WIDTH = ML_HEADS * ML_HEAD_DIM
ML_CHUNK = 128
ML_HALO = 8
CONV_WIDTH = 4
G_GROUPS = 4
G_GROUP_DIM = 128
G_WIDTH = G_GROUPS * G_GROUP_DIM
G_SPAN = 128
N_BRANCH = 3
X_HEADS = 4
X_HEAD_DIM = D_MODEL // X_HEADS
D_FF = 4 * D_MODEL
NORM_MIX_PRE, NORM_MIX_POST, NORM_X_PRE, NORM_X_POST, NORM_MEM, NORM_FF_PRE, NORM_FF_POST = range(7)
N_NORMS = 7

LOG2E = 1.4426950408889634
LANES = 128
MIB = 1024 * 1024

_IN_SPLITS = (
    ("fox_q", FOX_WIDTH), ("fox_k", FOX_WIDTH), ("fox_v", FOX_WIDTH), ("fox_f", FOX_HEADS),
    ("ml_q", ML_WIDTH), ("ml_k", ML_WIDTH), ("ml_v", ML_WIDTH),
    ("ml_i", ML_HEADS), ("ml_f", ML_HEADS), ("ml_o", ML_WIDTH),
    ("g_u", G_WIDTH), ("g_v", G_WIDTH),
    ("gate", N_BRANCH * D_MODEL),
)
_IN_OFFSETS = {}
_off = 0
for _name, _size in _IN_SPLITS:
    _IN_OFFSETS[_name] = (_off, _size)
    _off += _size

_WIDE_ORDER = ("gate", "fox_q", "fox_k", "fox_v", "ml_q", "ml_k", "ml_v", "ml_o", "g_u", "g_v")
_WIDE_OFFSETS = {}
_off = 0
for _name in _WIDE_ORDER:
    _WIDE_OFFSETS[_name] = _off
    _off += _IN_OFFSETS[_name][1]
N_WIDE = _off
_NARROW_ORDER = ("fox_f", "ml_i", "ml_f")
LANE_FOX_F = 0
LANE_ML_I = FOX_HEADS
LANE_ML_F = FOX_HEADS + ML_HEADS
N_GATE_LANES = FOX_HEADS + 2 * ML_HEADS


def _rms(x, g):
    return x * lax.rsqrt(jnp.mean(x * x, axis=-1, keepdims=True) + EPS) * g


def _log_sigmoid(x):
    return jnp.minimum(x, 0.0) - jnp.log1p(jnp.exp(-jnp.abs(x)))


def _params(semantics, vmem_mib):
    return pltpu.CompilerParams(dimension_semantics=semantics, vmem_limit_bytes=vmem_mib * MIB)


def _per_layer(block, layer, tail=None):
    block = tuple(block)
    if tail is None:
        tail = lambda *ids: (0,) * len(block)
    return pl.BlockSpec((None,) + block, lambda *ids: (layer,) + tuple(tail(*ids)))


def _in_proj_kernel(x_ref, g_ref, w_ref, b_ref, wn_ref, bn_ref, z_ref, zn_ref, h_ref):
    @pl.when(pl.program_id(1) == 0)
    def _():
        h = _rms(x_ref[...], g_ref[...]).astype(BF16)
        h_ref[...] = h
        zn_ref[...] = jnp.dot(h, wn_ref[...], preferred_element_type=F32) + bn_ref[...]

    z = jnp.dot(h_ref[...], w_ref[...], preferred_element_type=F32) + b_ref[...]
    z_ref[...] = z.astype(z_ref.dtype)


def in_proj(x2, gains, w_wide, b_wide, w_narrow, b_narrow, *, layer, gain, tm=2048, tn=1536):
    T, D = x2.shape
    N = w_wide.shape[-1]
    tm = min(tm, T)
    return pl.pallas_call(
        _in_proj_kernel,
        grid=(T // tm, N // tn),
        in_specs=[
            pl.BlockSpec((tm, D), lambda i, j: (i, 0)),
            _per_layer((1, D), gain),
            _per_layer((D, tn), layer, lambda i, j: (0, j)),
            _per_layer((1, tn), layer, lambda i, j: (0, j)),
            _per_layer((D, LANES), layer),
            _per_layer((1, LANES), layer),
        ],
        out_specs=[
            pl.BlockSpec((tm, tn), lambda i, j: (i, j)),
            pl.BlockSpec((tm, LANES), lambda i, j: (i, 0)),
        ],
        out_shape=[
            jax.ShapeDtypeStruct((T, N), BF16),
            jax.ShapeDtypeStruct((T, LANES), F32),
        ],
        scratch_shapes=[pltpu.VMEM((tm, D), BF16)],
        compiler_params=_params(("parallel", "arbitrary"), 56),
        name="in_proj",
    )(x2, gains, w_wide, b_wide, w_narrow, b_narrow)


def _gates_kernel(zn_ref, g_ref, *, rows):
    S = zn_ref.shape[0]
    lane = lax.broadcasted_iota(jnp.int32, (rows, LANES), 1)
    keep_raw = jnp.logical_and(lane >= LANE_ML_I, lane < LANE_ML_F)
    r = lax.broadcasted_iota(jnp.int32, (rows, rows), 0)
    c = lax.broadcasted_iota(jnp.int32, (rows, rows), 1)
    tril = jnp.where(r >= c, 1.0, 0.0).astype(F32)

    def body(i, carry):
        sl = pl.ds(pl.multiple_of(i * rows, rows), rows)
        z = zn_ref[sl, :]
        ls = _log_sigmoid(z)
        cs = jnp.dot(tril, ls, precision=lax.Precision.HIGHEST, preferred_element_type=F32) + carry
        g_ref[sl, :] = jnp.where(keep_raw, z, cs)
        return cs[rows - 1:rows, :]

    lax.fori_loop(0, S // rows, body, jnp.zeros((1, LANES), F32))


def gates(zn3, *, rows=256):
    B, S, _ = zn3.shape
    rows = min(rows, S)
    return pl.pallas_call(
        functools.partial(_gates_kernel, rows=rows),
        grid=(B,),
        in_specs=[pl.BlockSpec((None, S, LANES), lambda b: (b, 0, 0))],
        out_specs=pl.BlockSpec((None, S, LANES), lambda b: (b, 0, 0)),
        out_shape=jax.ShapeDtypeStruct((B, S, LANES), F32),
        compiler_params=_params(("parallel",), 16),
        name="gates",
    )(zn3)


FOX_AUG = LANES
FOX_PAIR = 2 * FOX_HEAD_DIM
FOX_VT_ROWS = FOX_HEAD_DIM + 16


def _fox_place_matrices():
    pk = np.zeros((3 * LANES, FOX_HEADS * FOX_AUG), np.float32)
    pq = np.zeros((3 * LANES, FOX_HEADS * FOX_AUG), np.float32)
    ck = np.zeros((1, FOX_HEADS * FOX_AUG), np.float32)
    cq = np.zeros((1, FOX_HEADS * FOX_AUG), np.float32)
    for h in range(FOX_HEADS):
        for piece in range(3):
            pk[piece * LANES + LANE_FOX_F + h, h * FOX_AUG + piece] = -1.0
            pq[piece * LANES + LANE_FOX_F + h, h * FOX_AUG + 3 + piece] = 1.0
            ck[0, h * FOX_AUG + 3 + piece] = 1.0
            cq[0, h * FOX_AUG + piece] = 1.0
    return (jnp.asarray(pk, BF16), jnp.asarray(pq, BF16), jnp.asarray(ck), jnp.asarray(cq))


def _fox_bias_lanes(f, place_ref, ones_ref):
    f = f * LOG2E
    hi = f.astype(BF16)
    r1 = f - hi.astype(F32)
    mid = r1.astype(BF16)
    lo = (r1 - mid.astype(F32)).astype(BF16)
    x = jnp.concatenate([hi, mid, lo], axis=-1)
    return (jnp.dot(x, place_ref[...], preferred_element_type=F32) + ones_ref[...]).astype(BF16)


def _fox_kernel(q_ref, k_ref, v_ref, g_ref, pk_ref, pq_ref, ck_ref, cq_ref, o_ref,
                kaug_ref, vt_ref, qaug_ref, m_ref, acc_ref, al_ref, st_ref, p_ref, *, blk):
    S = k_ref.shape[0]
    qi = pl.program_id(1)
    key_pos = lax.broadcasted_iota(jnp.int32, (blk, blk), 0)
    qry_pos = lax.broadcasted_iota(jnp.int32, (blk, blk), 1)
    causal = key_pos <= qry_pos

    @pl.when(qi == 0)
    def _():
        ones = jnp.ones((FOX_VT_ROWS - FOX_HEAD_DIM, blk), BF16)
        for c in range(S // blk):
            rows = slice(c * blk, (c + 1) * blk)
            kaug_ref[rows, :] = _fox_bias_lanes(g_ref[rows, :], pk_ref, ck_ref)
            for j in range(FOX_HEADS // 2):
                vt = v_ref[rows, j * FOX_PAIR:(j + 1) * FOX_PAIR].T
                for hh in range(2):
                    vt_ref[c, 2 * j + hh, :FOX_HEAD_DIM, :] = vt[hh * FOX_HEAD_DIM:(hh + 1) * FOX_HEAD_DIM, :]
                    vt_ref[c, 2 * j + hh, FOX_HEAD_DIM:, :] = ones

    q_rows = pl.ds(pl.multiple_of(qi * blk, blk), blk)
    qaug = _fox_bias_lanes(g_ref[q_rows, :], pq_ref, cq_ref)
    half = lax.broadcasted_iota(jnp.int32, (blk, FOX_PAIR), 1) // FOX_HEAD_DIM
    scale = FOX_HEAD_DIM ** -0.5 * LOG2E
    for h in range(FOX_HEADS):
        pair = slice((h // 2) * FOX_PAIR, (h // 2 + 1) * FOX_PAIR)
        qs = (q_ref[:, pair].astype(F32) * scale).astype(BF16)
        qm = jnp.where(half == h % 2, qs, jnp.zeros((), BF16))
        qaug_ref[h] = jnp.concatenate([qm, qaug[:, h * FOX_AUG:(h + 1) * FOX_AUG]], axis=-1).T
    m_ref[...] = jnp.full_like(m_ref, -jnp.inf)
    acc_ref[...] = jnp.zeros_like(acc_ref)

    def score_head(kb, slot, h):
        ks = pl.ds(pl.multiple_of(kb * blk, blk), blk)
        pair = slice((h // 2) * FOX_PAIR, (h // 2 + 1) * FOX_PAIR)
        kk = jnp.concatenate([k_ref[ks, pair], kaug_ref[ks, h * FOX_AUG:(h + 1) * FOX_AUG]], axis=-1)
        st_ref[slot, h] = jnp.dot(kk, qaug_ref[h], preferred_element_type=F32)

    def softmax_head(slot, h, masked):
        st = st_ref[slot, h]
        if masked:
            st = jnp.where(causal, st, -jnp.inf)
        m_old = m_ref[h:h + 1, :]
        m_new = jnp.maximum(m_old, jnp.max(st, axis=0, keepdims=True))
        al_ref[h:h + 1, :] = jnp.exp2(m_old - m_new)
        m_ref[h:h + 1, :] = m_new
        p_ref[h] = jnp.exp2(st - m_new).astype(BF16)

    def value_head(kb, h):
        pv = jnp.dot(vt_ref[kb, h], p_ref[h], preferred_element_type=F32)
        acc_ref[h] = al_ref[h:h + 1, :] * acc_ref[h] + pv

    def advance(kb, slot, masked, prefetch):
        for h in range(FOX_HEADS):
            if prefetch:
                score_head(kb + 1, 1 - slot, h)
            softmax_head(slot, h, masked)
            if h > 0:
                value_head(kb, h - 1)
        value_head(kb, FOX_HEADS - 1)

    for h in range(FOX_HEADS):
        score_head(0, 0, h)

    def two_blocks(i, carry):
        advance(2 * i, 0, False, True)
        advance(2 * i + 1, 1, False, True)
        return carry

    lax.fori_loop(0, qi // 2, two_blocks, 0)

    @pl.when(qi % 2 == 0)
    def _():
        advance(qi, 0, True, False)

    @pl.when(qi % 2 == 1)
    def _():
        advance(qi - 1, 0, False, True)
        advance(qi, 1, True, False)

    outs = []
    for j in range(FOX_HEADS // 2):
        tops = []
        for h in (2 * j, 2 * j + 1):
            a = acc_ref[h]
            tops.append(a[:FOX_HEAD_DIM, :] / a[FOX_HEAD_DIM:FOX_HEAD_DIM + 1, :])
        outs.append(jnp.concatenate(tops, axis=0).T)
    o_ref[...] = jnp.concatenate(outs, axis=-1).astype(o_ref.dtype)


def fox_attention(z3, g3, *, blk=256):
    B, S, _ = z3.shape
    wq = _WIDE_OFFSETS["fox_q"] // FOX_WIDTH
    wk = _WIDE_OFFSETS["fox_k"] // FOX_WIDTH
    wv = _WIDE_OFFSETS["fox_v"] // FOX_WIDTH
    pk, pq, ck, cq = _fox_place_matrices()
    const = lambda a: pl.BlockSpec(a.shape, lambda b, i: (0,) * a.ndim)
    return pl.pallas_call(
        functools.partial(_fox_kernel, blk=blk),
        grid=(B, S // blk),
        in_specs=[
            pl.BlockSpec((None, blk, FOX_WIDTH), lambda b, i: (b, i, wq)),
            pl.BlockSpec((None, S, FOX_WIDTH), lambda b, i: (b, 0, wk)),
            pl.BlockSpec((None, S, FOX_WIDTH), lambda b, i: (b, 0, wv)),
            pl.BlockSpec((None, S, LANES), lambda b, i: (b, 0, 0)),
            const(pk), const(pq), const(ck), const(cq),
        ],
        out_specs=pl.BlockSpec((None, blk, FOX_WIDTH), lambda b, i: (b, i, 0)),
        out_shape=jax.ShapeDtypeStruct((B, S, FOX_WIDTH), BF16),
        scratch_shapes=[
            pltpu.VMEM((S, FOX_HEADS * FOX_AUG), BF16),
            pltpu.VMEM((S // blk, FOX_HEADS, FOX_VT_ROWS, blk), BF16),
            pltpu.VMEM((FOX_HEADS, FOX_PAIR + FOX_AUG, blk), BF16),
            pltpu.VMEM((FOX_HEADS, blk), F32),
            pltpu.VMEM((FOX_HEADS, FOX_VT_ROWS, blk), F32),
            pltpu.VMEM((FOX_HEADS, blk), F32),
            pltpu.VMEM((2, FOX_HEADS, blk, blk), F32),
            pltpu.VMEM((FOX_HEADS, blk, blk), BF16),
        ],
        compiler_params=_params(("parallel", "arbitrary"), 48),
        name="fox",
    )(z3, z3, z3, g3, pk, pq, ck, cq)


def _mlstm_kernel(q_ref, k_ref, v_ref, o_ref, cw_ref, gn_ref, g_ref, gt_ref, y_ref,
                  cn_ref, st_ref, halo_ref, gnb_ref, *, L):
    S = q_ref.shape[0]
    HALO = halo_ref.shape[0]
    src = lax.broadcasted_iota(jnp.int32, (L, L), 0)
    qry = lax.broadcasted_iota(jnp.int32, (L, L), 1)
    triu = src <= qry
    lane = lax.broadcasted_iota(jnp.int32, (L, LANES), 1)
    ones_col = jnp.where(lane == 0, 1.0, 0.0).astype(BF16)
    k_scale = ML_HEAD_DIM ** -0.5
    dn_t = (((1,), (1,)), ((), ()))
    dn_0 = (((0,), (0,)), ((), ()))

    cn_ref[...] = jnp.zeros_like(cn_ref)
    st_ref[...] = jnp.zeros_like(st_ref)
    halo_ref[...] = jnp.zeros_like(halo_ref)
    for h in range(ML_HEADS):
        gnb_ref[h] = jnp.broadcast_to(gn_ref[h * ML_HEAD_DIM:(h + 1) * ML_HEAD_DIM, :], (ML_HEAD_DIM, L))

    def conv_silu(x_chunk, halo, w):
        xx = jnp.concatenate([halo, x_chunk], axis=0)
        y = w[CONV_WIDTH - 1:CONV_WIDTH, :] * x_chunk
        for j in range(CONV_WIDTH - 1):
            sh = CONV_WIDTH - 1 - j
            y = y + w[j:j + 1, :] * xx[HALO - sh:HALO - sh + L, :]
        return y * jax.nn.sigmoid(y)

    def chunk(ci, _):
        rows = pl.ds(pl.multiple_of(ci * L, L), L)
        xq = q_ref[rows, :].astype(F32)
        xk = k_ref[rows, :].astype(F32)
        qa = conv_silu(xq, halo_ref[:, :ML_WIDTH], cw_ref[:, :ML_WIDTH])
        ka = conv_silu(xk, halo_ref[:, ML_WIDTH:], cw_ref[:, ML_WIDTH:]) * k_scale
        halo_ref[:, :ML_WIDTH] = xq[L - HALO:, :]
        halo_ref[:, ML_WIDTH:] = xk[L - HALO:, :]
        gcol = g_ref[rows, :]
        grow = gt_ref[ci]

        for h in range(ML_HEADS):
            cols = slice(h * ML_HEAD_DIM, (h + 1) * ML_HEAD_DIM)
            qb = qa[:, cols].astype(BF16)
            kf = ka[:, cols]
            vb = v_ref[rows, cols]
            c_col = gcol[:, LANE_ML_I + h:LANE_ML_I + h + 1] - gcol[:, LANE_ML_F + h:LANE_ML_F + h + 1]
            f_r = grow[LANE_ML_F + h:LANE_ML_F + h + 1, :]
            f_prev = st_ref[h, 0:1, 0:1]
            m_prev = st_ref[h, 0:1, 1:2]

            d_t = jnp.where(triu, c_col + f_r, -jnp.inf)
            inter = f_r - f_prev + m_prev
            m = jnp.maximum(inter, jnp.max(d_t, axis=0, keepdims=True))
            w_inter = jnp.exp(inter - m)
            p_t = lax.dot_general(kf.astype(BF16), qb, dn_t, preferred_element_type=F32) * jnp.exp(d_t - m)
            io_t = lax.dot_general(cn_ref[h].astype(BF16), qb, dn_t, preferred_element_type=F32)
            pv_t = lax.dot_general(vb, p_t.astype(BF16), dn_0, preferred_element_type=F32)
            num = w_inter * io_t[:ML_HEAD_DIM, :] + pv_t
            den = w_inter * io_t[ML_HEAD_DIM:ML_HEAD_DIM + 1, :] + jnp.sum(p_t, axis=0, keepdims=True)
            h_t = num / jnp.maximum(jnp.abs(den), jnp.exp(-m))

            m_new = m[:, L - 1:L]
            f_end = f_r[:, L - 1:L]
            decay = jnp.exp(f_end - f_prev + m_prev - m_new)
            w_s = jnp.exp(jnp.broadcast_to(c_col, (L, ML_HEAD_DIM)) + (f_end - m_new))
            kw = (w_s * kf).astype(BF16)
            v_aug = jnp.concatenate([vb, ones_col], axis=-1)
            upd = lax.dot_general(v_aug, kw, dn_0, preferred_element_type=F32)
            cn_ref[h] = decay * cn_ref[h] + upd
            st_ref[h, 0:1, 0:1] = f_end
            st_ref[h, 0:1, 1:2] = m_new

            hn_t = h_t * lax.rsqrt(jnp.mean(h_t * h_t, axis=0, keepdims=True) + EPS) * gnb_ref[h]
            y = jax.nn.sigmoid(o_ref[rows, cols].astype(F32)) * hn_t.T
            y_ref[rows, cols] = y.astype(y_ref.dtype)
        return 0

    lax.fori_loop(0, S // L, chunk, 0)


def mlstm(z3, conv_w, mlstm_g, g3, gt4, *, layer, L=CHUNK):
    B, S, _ = z3.shape
    blocks = [_WIDE_OFFSETS[n] // ML_WIDTH for n in ("ml_q", "ml_k", "ml_v", "ml_o")]
    seq_spec = lambda idx: pl.BlockSpec((None, S, ML_WIDTH), lambda b: (b, 0, idx))
    return pl.pallas_call(
        functools.partial(_mlstm_kernel, L=L),
        grid=(B,),
        in_specs=[
            seq_spec(blocks[0]), seq_spec(blocks[1]), seq_spec(blocks[2]), seq_spec(blocks[3]),
            _per_layer((CONV_WIDTH, 2 * ML_WIDTH), layer),
            _per_layer((ML_WIDTH, 1), layer),
            pl.BlockSpec((None, S, LANES), lambda b: (b, 0, 0)),
            pl.BlockSpec((None, S // L, N_GATE_LANES, L), lambda b: (b, 0, 0, 0)),
        ],
        out_specs=pl.BlockSpec((None, S, ML_WIDTH), lambda b: (b, 0, 0)),
        out_shape=jax.ShapeDtypeStruct((B, S, ML_WIDTH), BF16),
        scratch_shapes=[
            pltpu.VMEM((ML_HEADS, 2 * ML_HEAD_DIM, ML_HEAD_DIM), F32),
            pltpu.VMEM((ML_HEADS, 8, LANES), F32),
            pltpu.VMEM((ML_HALO, 2 * ML_WIDTH), F32),
            pltpu.VMEM((ML_HEADS, ML_HEAD_DIM, L), F32),
        ],
        compiler_params=_params(("parallel",), 40),
        name="mlstm",
    )(z3, z3, z3, z3, conv_w, mlstm_g, g3, gt4)


def _gmlp_kernel(u_ref, v_ref, gn_ref, ws_ref, bst_ref, y_ref):
    rows = u_ref.shape[0]
    u = jax.nn.gelu(u_ref[...].astype(F32))
    v = jax.nn.gelu(v_ref[...].astype(F32))
    mu = jnp.mean(v, axis=-1, keepdims=True)
    vc = v - mu
    vn = (vc * lax.rsqrt(jnp.mean(vc * vc, axis=-1, keepdims=True) + EPS) * gn_ref[...]).astype(BF16)
    r = lax.broadcasted_iota(jnp.int32, (G_SPAN, G_SPAN), 0) // CHUNK
    c = lax.broadcasted_iota(jnp.int32, (G_SPAN, G_SPAN), 1) // CHUNK
    mask = r >= c
    for g in range(G_GROUPS):
        cols = slice(g * G_GROUP_DIM, (g + 1) * G_GROUP_DIM)
        w = jnp.where(mask, ws_ref[g], 0.0).astype(BF16)
        bias = bst_ref[:, g:g + 1]
        for s in range(rows // G_SPAN):
            rs = slice(s * G_SPAN, (s + 1) * G_SPAN)
            mixed = jnp.dot(w, vn[rs, cols], preferred_element_type=F32) + bias
            y_ref[rs, cols] = (u[rs, cols] * mixed).astype(y_ref.dtype)


def gmlp(z3, gmlp_g, ws, bs_t, *, layer, rows=512):
    B, S, _ = z3.shape
    rows = min(rows, S)
    bu = _WIDE_OFFSETS["g_u"] // G_WIDTH
    bv = _WIDE_OFFSETS["g_v"] // G_WIDTH
    return pl.pallas_call(
        _gmlp_kernel,
        grid=(B, S // rows),
        in_specs=[
            pl.BlockSpec((None, rows, G_WIDTH), lambda b, i: (b, i, bu)),
            pl.BlockSpec((None, rows, G_WIDTH), lambda b, i: (b, i, bv)),
            _per_layer((1, G_WIDTH), layer),
            _per_layer((G_GROUPS, G_SPAN, G_SPAN), layer),
            _per_layer((G_SPAN, G_GROUPS), layer),
        ],
        out_specs=pl.BlockSpec((None, rows, G_WIDTH), lambda b, i: (b, i, 0)),
        out_shape=jax.ShapeDtypeStruct((B, S, G_WIDTH), BF16),
        compiler_params=_params(("parallel", "parallel"), 16),
        name="gmlp",
    )(z3, z3, gmlp_g, ws, bs_t)


def _merge_kernel(x_ref, gate_ref, ya_ref, yb_ref, yc_ref, wb_ref, wo_ref, gpost_ref, o_ref):
    merged = None
    for n, y_ref in enumerate((ya_ref, yb_ref, yc_ref)):
        br = jnp.dot(y_ref[...], wb_ref[n], preferred_element_type=F32)
        gt = jax.nn.sigmoid(gate_ref[:, n * D_MODEL:(n + 1) * D_MODEL].astype(F32))
        merged = gt * br if merged is None else merged + gt * br
    y = jnp.dot(merged.astype(BF16), wo_ref[...], preferred_element_type=F32)
    o_ref[...] = x_ref[...] + _rms(y, gpost_ref[...])


def merge(x2, z2, y_fox, y_ml, y_g, w_branch, w_out, gains, *, layer, gain, tm=512):
    T, D = x2.shape
    tm = min(tm, T)
    row = lambda w: pl.BlockSpec((tm, w), lambda i: (i, 0))
    return pl.pallas_call(
        _merge_kernel,
        grid=(T // tm,),
        in_specs=[
            row(D),
            row(N_BRANCH * D),
            row(FOX_WIDTH), row(ML_WIDTH), row(G_WIDTH),
            _per_layer((N_BRANCH, FOX_WIDTH, D), layer),
            _per_layer((D, D), layer),
            _per_layer((1, D), gain),
        ],
        out_specs=row(D),
        out_shape=jax.ShapeDtypeStruct((T, D), F32),
        compiler_params=_params(("parallel",), 48),
        name="merge",
    )(x2, z2, y_fox, y_ml, y_g, w_branch, w_out, gains)


def _norm_matmul_kernel(x_ref, g_ref, w_ref, o_ref):
    h = _rms(x_ref[...], g_ref[...]).astype(BF16)
    o_ref[...] = jnp.dot(h, w_ref[...], preferred_element_type=F32).astype(o_ref.dtype)


def norm_matmul(x2, gains, w, *, layer, gain, tm=512):
    T, D = x2.shape
    N = w.shape[-1]
    tm = min(tm, T)
    return pl.pallas_call(
        _norm_matmul_kernel,
        grid=(T // tm,),
        in_specs=[
            pl.BlockSpec((tm, D), lambda i: (i, 0)),
            _per_layer((1, D), gain),
            _per_layer((D, N), layer),
        ],
        out_specs=pl.BlockSpec((tm, N), lambda i: (i, 0)),
        out_shape=jax.ShapeDtypeStruct((T, N), BF16),
        compiler_params=_params(("parallel",), 40),
        name="mem_kv",
    )(x2, gains, w)


def _xattn_kernel(x_ref, kv_ref, wq_ref, wo_ref, gpre_ref, gpost_ref, o_ref):
    x = x_ref[...]
    h = _rms(x, gpre_ref[...]).astype(BF16)
    q = jnp.dot(h, wq_ref[...], preferred_element_type=F32).astype(BF16)
    dn = (((1,), (1,)), ((), ()))
    outs = []
    for hd in range(X_HEADS):
        cols = slice(hd * X_HEAD_DIM, (hd + 1) * X_HEAD_DIM)
        k = kv_ref[:, cols]
        v = kv_ref[:, D_MODEL + hd * X_HEAD_DIM:D_MODEL + (hd + 1) * X_HEAD_DIM]
        s = lax.dot_general(q[:, cols], k, dn, preferred_element_type=F32) * (X_HEAD_DIM ** -0.5)
        s = s - jnp.max(s, axis=-1, keepdims=True)
        e = jnp.exp(s)
        p = e / jnp.sum(e, axis=-1, keepdims=True)
        outs.append(jnp.dot(p.astype(BF16), v, preferred_element_type=F32).astype(BF16))
    o = jnp.concatenate(outs, axis=-1)
    y = jnp.dot(o, wo_ref[...], preferred_element_type=F32)
    o_ref[...] = x + _rms(y, gpost_ref[...])


def xattn(x3, kv3, w_q, w_o, gains, *, layer, gain_pre, gain_post, tm=512):
    B, S, D = x3.shape
    M = kv3.shape[1]
    tm = min(tm, S)
    return pl.pallas_call(
        _xattn_kernel,
        grid=(B, S // tm),
        in_specs=[
            pl.BlockSpec((None, tm, D), lambda b, i: (b, i, 0)),
            pl.BlockSpec((None, M, 2 * D), lambda b, i: (b, 0, 0)),
            _per_layer((D, D), layer), _per_layer((D, D), layer),
            _per_layer((1, D), gain_pre), _per_layer((1, D), gain_post),
        ],
        out_specs=pl.BlockSpec((None, tm, D), lambda b, i: (b, i, 0)),
        out_shape=jax.ShapeDtypeStruct((B, S, D), F32),
        compiler_params=_params(("parallel", "arbitrary"), 48),
        name="xattn",
    )(x3, kv3, w_q, w_o, gains, gains)


def _ffn_kernel(x_ref, w1_ref, w2_ref, gpre_ref, gpost_ref, o_ref, *, ff_chunk):
    x = x_ref[...]
    h = _rms(x, gpre_ref[...]).astype(BF16)
    acc = None
    for c in range(w1_ref.shape[1] // ff_chunk):
        cs = slice(c * ff_chunk, (c + 1) * ff_chunk)
        a = jnp.dot(h, w1_ref[:, cs], preferred_element_type=F32)
        a = jnp.square(jnp.maximum(a, 0.0)).astype(BF16)
        part = jnp.dot(a, w2_ref[cs, :], preferred_element_type=F32)
        acc = part if acc is None else acc + part
    o_ref[...] = x + _rms(acc, gpost_ref[...])


def ffn(x2, w1, w2, gains, *, layer, gain_pre, gain_post, tm=512, ff_chunk=1024):
    T, D = x2.shape
    FF = w1.shape[-1]
    tm = min(tm, T)
    return pl.pallas_call(
        functools.partial(_ffn_kernel, ff_chunk=ff_chunk),
        grid=(T // tm,),
        in_specs=[
            pl.BlockSpec((tm, D), lambda i: (i, 0)),
            _per_layer((D, FF), layer),
            _per_layer((FF, D), layer),
            _per_layer((1, D), gain_pre),
            _per_layer((1, D), gain_post),
        ],
        out_specs=pl.BlockSpec((tm, D), lambda i: (i, 0)),
        out_shape=jax.ShapeDtypeStruct((T, D), F32),
        compiler_params=_params(("parallel",), 56),
        name="ffn",
    )(x2, w1, w2, gains, gains)


def _regroup_in_proj(w_in, b_in):
    def cols(names):
        return [slice(_IN_OFFSETS[n][0], _IN_OFFSETS[n][0] + _IN_OFFSETS[n][1]) for n in names]
    w_wide = jnp.concatenate([w_in[..., s] for s in cols(_WIDE_ORDER)], axis=-1).astype(BF16)
    b_wide = jnp.concatenate([b_in[..., s] for s in cols(_WIDE_ORDER)], axis=-1)[:, None, :]
    pad = LANES - N_GATE_LANES
    w_narrow = jnp.pad(jnp.concatenate([w_in[..., s] for s in cols(_NARROW_ORDER)], axis=-1),
                       ((0, 0), (0, 0), (0, pad))).astype(BF16)
    b_narrow = jnp.pad(jnp.concatenate([b_in[..., s] for s in cols(_NARROW_ORDER)], axis=-1),
                       ((0, 0), (0, pad)))[:, None, :]
    return w_wide, b_wide, w_narrow, b_narrow


def _time_on_lanes(g3, blk):
    B, S, _ = g3.shape
    return g3[:, :, :N_GATE_LANES].reshape(B, S // blk, blk, N_GATE_LANES).transpose(0, 1, 3, 2)


def _prepare_params(norms, w_in, b_in, conv_w, mlstm_norm, gmlp_norm, gmlp_ws, gmlp_bs,
                    w_branch, w_out, w_xq, w_xkv, w_xo, w_ff1, w_ff2):
    depth = norms.shape[0]
    w_wide, b_wide, w_narrow, b_narrow = _regroup_in_proj(w_in, b_in)
    return dict(
        gains=norms.reshape(depth * N_NORMS, 1, D_MODEL),
        w_wide=w_wide, b_wide=b_wide, w_narrow=w_narrow, b_narrow=b_narrow,
        conv_w=conv_w, mlstm_g=mlstm_norm[:, :, None], gmlp_g=gmlp_norm[:, None, :],
        gmlp_ws=gmlp_ws, gmlp_bs_t=gmlp_bs.transpose(0, 2, 1),
        w_branch=w_branch.astype(BF16), w_out=w_out.astype(BF16),
        w_xq=w_xq.astype(BF16), w_xkv=w_xkv.astype(BF16), w_xo=w_xo.astype(BF16),
        w_ff1=w_ff1.astype(BF16), w_ff2=w_ff2.astype(BF16),
    )


def _layer(x3, mem2, p, layer, *, fox_blk, ml_chunk):
    B, S, D = x3.shape
    T = B * S
    M = mem2.shape[0] // B
    gain = lambda idx: layer * N_NORMS + idx
    x2 = x3.reshape(T, D)

    z2, zn2 = in_proj(x2, p["gains"], p["w_wide"], p["b_wide"], p["w_narrow"], p["b_narrow"],
                      layer=layer, gain=gain(NORM_MIX_PRE))
    z3 = z2.reshape(B, S, N_WIDE)
    g3 = gates(zn2.reshape(B, S, LANES))

    y_fox = fox_attention(z3, g3, blk=fox_blk)
    y_ml = mlstm(z3, p["conv_w"], p["mlstm_g"], g3, _time_on_lanes(g3, ml_chunk), layer=layer, L=ml_chunk)
    y_g = gmlp(z3, p["gmlp_g"], p["gmlp_ws"], p["gmlp_bs_t"], layer=layer)
    x2 = merge(x2, z2, y_fox.reshape(T, FOX_WIDTH), y_ml.reshape(T, ML_WIDTH), y_g.reshape(T, G_WIDTH),
               p["w_branch"], p["w_out"], p["gains"], layer=layer, gain=gain(NORM_MIX_POST))

    kv = norm_matmul(mem2, p["gains"], p["w_xkv"], layer=layer, gain=gain(NORM_MEM))
    x3 = xattn(x2.reshape(B, S, D), kv.reshape(B, M, 2 * D), p["w_xq"], p["w_xo"], p["gains"],
               layer=layer, gain_pre=gain(NORM_X_PRE), gain_post=gain(NORM_X_POST))

    x2 = ffn(x3.reshape(T, D), p["w_ff1"], p["w_ff2"], p["gains"],
             layer=layer, gain_pre=gain(NORM_FF_PRE), gain_post=gain(NORM_FF_POST))
    return x2.reshape(B, S, D)


def kernel(x, mem, norms, w_in, b_in, conv_w, mlstm_norm, gmlp_norm, gmlp_ws, gmlp_bs, w_branch, w_out, w_xq, w_xkv, w_xo, w_ff1, w_ff2):
    B, M, D = mem.shape
    S = x.shape[1]
    mem2 = mem.reshape(B * M, D)
    p = _prepare_params(norms, w_in, b_in, conv_w, mlstm_norm, gmlp_norm, gmlp_ws, gmlp_bs,
                        w_branch, w_out, w_xq, w_xkv, w_xo, w_ff1, w_ff2)
    for layer in range(norms.shape[0]):
        x = _layer(x, mem2, p, layer, fox_blk=min(256, S), ml_chunk=min(ML_CHUNK, S))
    return x
```

```python
import functools

import jax
import jax.numpy as jnp
import numpy as np
from jax import lax
from jax.experimental import pallas as pl
from jax.experimental.pallas import tpu as pltpu

F32 = jnp.float32
BF16 = jnp.bfloat16

EPS = 1e-6
D_MODEL = 1024
CHUNK = 64
FOX_HEADS = 8
FOX_HEAD_DIM = 64
FOX_WIDTH = FOX_HEADS * FOX_HEAD_DIM
ML_HEADS = 4
ML_HEAD_DIM = 128
ML_WIDTH = ML_HEADS * ML_HEAD_DIM
ML_CHUNK = 128
ML_HALO = 8
CONV_WIDTH = 4
G_GROUPS = 4
G_GROUP_DIM = 128
G_WIDTH = G_GROUPS * G_GROUP_DIM
G_SPAN = 128
N_BRANCH = 3
X_HEADS = 4
X_HEAD_DIM = D_MODEL // X_HEADS
D_FF = 4 * D_MODEL
NORM_MIX_PRE, NORM_MIX_POST, NORM_X_PRE, NORM_X_POST, NORM_MEM, NORM_FF_PRE, NORM_FF_POST = range(7)
N_NORMS = 7

LOG2E = 1.4426950408889634
LANES = 128
MIB = 1024 * 1024

_IN_SPLITS = (
    ("fox_q", FOX_WIDTH), ("fox_k", FOX_WIDTH), ("fox_v", FOX_WIDTH), ("fox_f", FOX_HEADS),
    ("ml_q", ML_WIDTH), ("ml_k", ML_WIDTH), ("ml_v", ML_WIDTH),
    ("ml_i", ML_HEADS), ("ml_f", ML_HEADS), ("ml_o", ML_WIDTH),
    ("g_u", G_WIDTH), ("g_v", G_WIDTH),
    ("gate", N_BRANCH * D_MODEL),
)
_IN_OFFSETS = {}
_off = 0
for _name, _size in _IN_SPLITS:
    _IN_OFFSETS[_name] = (_off, _size)
    _off += _size

_WIDE_ORDER = ("gate", "fox_q", "fox_k", "fox_v", "ml_q", "ml_k", "ml_v", "ml_o", "g_u", "g_v")
_WIDE_OFFSETS = {}
_off = 0
for _name in _WIDE_ORDER:
    _WIDE_OFFSETS[_name] = _off
    _off += _IN_OFFSETS[_name][1]
N_WIDE = _off
_NARROW_ORDER = ("fox_f", "ml_i", "ml_f")
LANE_FOX_F = 0
LANE_ML_I = FOX_HEADS
LANE_ML_F = FOX_HEADS + ML_HEADS
N_GATE_LANES = FOX_HEADS + 2 * ML_HEADS


def _rms(x, g):
    return x * lax.rsqrt(jnp.mean(x * x, axis=-1, keepdims=True) + EPS) * g


def _log_sigmoid(x):
    return jnp.minimum(x, 0.0) - jnp.log1p(jnp.exp(-jnp.abs(x)))


def _params(semantics, vmem_mib):
    return pltpu.CompilerParams(dimension_semantics=semantics, vmem_limit_bytes=vmem_mib * MIB)


def _per_layer(block, layer, tail=None):
    block = tuple(block)
    if tail is None:
        return pl.BlockSpec((None,) + block, lambda *ids: (layer,) + (0,) * len(block),
                            pipeline_mode=pl.Buffered(1))
    return pl.BlockSpec((None,) + block, lambda *ids: (layer,) + tuple(tail(*ids)))


def _in_proj_kernel(x_ref, g_ref, w_ref, b_ref, wn_ref, bn_ref, z_ref, zn_ref, h_ref):
    @pl.when(pl.program_id(1) == 0)
    def _():
        h = _rms(x_ref[...], g_ref[...]).astype(BF16)
        h_ref[...] = h
        zn_ref[...] = jnp.dot(h, wn_ref[...], preferred_element_type=F32) + bn_ref[...]

    z = jnp.dot(h_ref[...], w_ref[...], preferred_element_type=F32) + b_ref[...]
    z_ref[...] = z.astype(z_ref.dtype)


def in_proj(x2, gains, w_wide, b_wide, w_narrow, b_narrow, *, layer, gain, tm=2048, tn=1536):
    T, D = x2.shape
    N = w_wide.shape[-1]
    tm = min(tm, T)
    return pl.pallas_call(
        _in_proj_kernel,
        grid=(T // tm, N // tn),
        in_specs=[
            pl.BlockSpec((tm, D), lambda i, j: (i, 0)),
            _per_layer((1, D), gain),
            _per_layer((D, tn), layer, lambda i, j: (0, j)),
            _per_layer((1, tn), layer, lambda i, j: (0, j)),
            _per_layer((D, LANES), layer),
            _per_layer((1, LANES), layer),
        ],
        out_specs=[
            pl.BlockSpec((tm, tn), lambda i, j: (i, j)),
            pl.BlockSpec((tm, LANES), lambda i, j: (i, 0)),
        ],
        out_shape=[
            jax.ShapeDtypeStruct((T, N), BF16),
            jax.ShapeDtypeStruct((T, LANES), F32),
        ],
        scratch_shapes=[pltpu.VMEM((tm, D), BF16)],
        compiler_params=_params(("parallel", "arbitrary"), 56),
        name="in_proj",
    )(x2, gains, w_wide, b_wide, w_narrow, b_narrow)


def _gates_kernel(zn_ref, g_ref, *, rows):
    S = zn_ref.shape[0]
    lane = lax.broadcasted_iota(jnp.int32, (rows, LANES), 1)
    keep_raw = jnp.logical_and(lane >= LANE_ML_I, lane < LANE_ML_F)
    r = lax.broadcasted_iota(jnp.int32, (rows, rows), 0)
    c = lax.broadcasted_iota(jnp.int32, (rows, rows), 1)
    tril = jnp.where(r >= c, 1.0, 0.0).astype(F32)

    offset = jnp.zeros((1, LANES), F32)
    for i in range(S // rows):
        sl = slice(i * rows, (i + 1) * rows)
        z = zn_ref[sl, :]
        ls = _log_sigmoid(z)
        cs = jnp.dot(tril, ls, precision=lax.Precision.HIGHEST, preferred_element_type=F32) + offset
        g_ref[sl, :] = jnp.where(keep_raw, z, cs)
        offset = cs[rows - 1:rows, :]


def gates(zn3, *, rows=256):
    B, S, _ = zn3.shape
    rows = min(rows, S)
    return pl.pallas_call(
        functools.partial(_gates_kernel, rows=rows),
        grid=(B,),
        in_specs=[pl.BlockSpec((None, S, LANES), lambda b: (b, 0, 0))],
        out_specs=pl.BlockSpec((None, S, LANES), lambda b: (b, 0, 0)),
        out_shape=jax.ShapeDtypeStruct((B, S, LANES), F32),
        compiler_params=_params(("parallel",), 16),
        name="gates",
    )(zn3)


FOX_AUG = LANES
FOX_SUB = 8
FOX_PAIR = 2 * FOX_HEAD_DIM
FOX_PAIRS = FOX_HEADS // 2
FOX_VT_ROWS = FOX_HEAD_DIM + 16


def _fox_place_matrices():
    pk = np.zeros((3 * LANES, FOX_PAIRS * FOX_AUG), np.float32)
    pq = np.zeros((3 * LANES, FOX_PAIRS * FOX_AUG), np.float32)
    ck = np.zeros((1, FOX_PAIRS * FOX_AUG), np.float32)
    cq = np.zeros((1, FOX_PAIRS * FOX_AUG), np.float32)
    for h in range(FOX_HEADS):
        base = (h // 2) * FOX_AUG + (h % 2) * FOX_SUB
        for piece in range(3):
            pk[piece * LANES + LANE_FOX_F + h, base + piece] = -1.0
            pq[piece * LANES + LANE_FOX_F + h, base + 3 + piece] = 1.0
            ck[0, base + 3 + piece] = 1.0
            cq[0, base + piece] = 1.0
    return (jnp.asarray(pk, BF16), jnp.asarray(pq, BF16), jnp.asarray(ck), jnp.asarray(cq))


def _fox_bias_lanes(f, place_ref, ones_ref):
    f = f * LOG2E
    hi = f.astype(BF16)
    r1 = f - hi.astype(F32)
    mid = r1.astype(BF16)
    lo = (r1 - mid.astype(F32)).astype(BF16)
    x = jnp.concatenate([hi, mid, lo], axis=-1)
    return (jnp.dot(x, place_ref[...], preferred_element_type=F32) + ones_ref[...]).astype(BF16)


def _fox_kernel(q_ref, k_ref, v_ref, g_ref, pk_ref, pq_ref, ck_ref, cq_ref, o_ref,
                kaug_ref, vt_ref, qaug_ref, m_ref, acc_ref, al_ref, st_ref, p_ref, *, blk):
    S = k_ref.shape[0]
    qi = pl.program_id(1)
    key_pos = lax.broadcasted_iota(jnp.int32, (blk, blk), 0)
    qry_pos = lax.broadcasted_iota(jnp.int32, (blk, blk), 1)
    causal = key_pos <= qry_pos

    @pl.when(qi == 0)
    def _():
        ones = jnp.ones((FOX_VT_ROWS - FOX_HEAD_DIM, blk), BF16)
        for c in range(S // blk):
            rows = slice(c * blk, (c + 1) * blk)
            kaug_ref[rows, :] = _fox_bias_lanes(g_ref[rows, :], pk_ref, ck_ref)
            for j in range(FOX_PAIRS):
                vt = v_ref[rows, j * FOX_PAIR:(j + 1) * FOX_PAIR].T
                for hh in range(2):
                    vt_ref[c, 2 * j + hh, :FOX_HEAD_DIM, :] = vt[hh * FOX_HEAD_DIM:(hh + 1) * FOX_HEAD_DIM, :]
                    vt_ref[c, 2 * j + hh, FOX_HEAD_DIM:, :] = ones

    q_rows = pl.ds(pl.multiple_of(qi * blk, blk), blk)
    qaug = _fox_bias_lanes(g_ref[q_rows, :], pq_ref, cq_ref)
    q_half = lax.broadcasted_iota(jnp.int32, (blk, FOX_PAIR), 1) // FOX_HEAD_DIM
    aug_half = lax.broadcasted_iota(jnp.int32, (blk, FOX_AUG), 1) // FOX_SUB
    zero = jnp.zeros((), BF16)
    scale = FOX_HEAD_DIM ** -0.5 * LOG2E
    for j in range(FOX_PAIRS):
        qs = (q_ref[:, j * FOX_PAIR:(j + 1) * FOX_PAIR].astype(F32) * scale).astype(BF16)
        qa = qaug[:, j * FOX_AUG:(j + 1) * FOX_AUG]
        both = [jnp.concatenate([jnp.where(q_half == hh, qs, zero), jnp.where(aug_half == hh, qa, zero)], axis=-1).T
                for hh in range(2)]
        qaug_ref[j] = jnp.concatenate(both, axis=-1)
    m_ref[...] = jnp.full_like(m_ref, -jnp.inf)
    acc_ref[...] = jnp.zeros_like(acc_ref)

    def score_pair(kb, slot, j):
        ks = pl.ds(pl.multiple_of(kb * blk, blk), blk)
        kk = jnp.concatenate([k_ref[ks, j * FOX_PAIR:(j + 1) * FOX_PAIR],
                              kaug_ref[ks, j * FOX_AUG:(j + 1) * FOX_AUG]], axis=-1)
        st_ref[slot, j] = jnp.dot(kk, qaug_ref[j], preferred_element_type=F32)

    def softmax_head(slot, h, masked):
        st = st_ref[slot, h // 2, :, (h % 2) * blk:(h % 2 + 1) * blk]
        if masked:
            st = jnp.where(causal, st, -jnp.inf)
        m_old = m_ref[h:h + 1, :]
        m_new = jnp.maximum(m_old, jnp.max(st, axis=0, keepdims=True))
        al_ref[h:h + 1, :] = jnp.exp2(m_old - m_new)
        m_ref[h:h + 1, :] = m_new
        p_ref[h] = jnp.exp2(st - m_new).astype(BF16)

    def value_head(kb, h):
        pv = jnp.dot(vt_ref[kb, h], p_ref[h], preferred_element_type=F32)
        acc_ref[h] = al_ref[h:h + 1, :] * acc_ref[h] + pv

    def advance(kb, slot, masked, prefetch):
        for j in range(FOX_PAIRS):
            if prefetch:
                score_pair(kb + 1, 1 - slot, j)
            softmax_head(slot, 2 * j, masked)
            softmax_head(slot, 2 * j + 1, masked)
            if j > 0:
                value_head(kb, 2 * j - 2)
                value_head(kb, 2 * j - 1)
        value_head(kb, FOX_HEADS - 2)
        value_head(kb, FOX_HEADS - 1)

    for j in range(FOX_PAIRS):
        score_pair(0, 0, j)

    def two_blocks(i, carry):
        advance(2 * i, 0, False, True)
        advance(2 * i + 1, 1, False, True)
        return carry

    lax.fori_loop(0, qi // 2, two_blocks, 0)

    @pl.when(qi % 2 == 0)
    def _():
        advance(qi, 0, True, False)

    @pl.when(qi % 2 == 1)
    def _():
        advance(qi - 1, 0, False, True)
        advance(qi, 1, True, False)

    outs = []
    for j in range(FOX_HEADS // 2):
        tops = []
        for h in (2 * j, 2 * j + 1):
            a = acc_ref[h]
            tops.append(a[:FOX_HEAD_DIM, :] / a[FOX_HEAD_DIM:FOX_HEAD_DIM + 1, :])
        outs.append(jnp.concatenate(tops, axis=0).T)
    o_ref[...] = jnp.concatenate(outs, axis=-1).astype(o_ref.dtype)


def fox_attention(z3, g3, *, blk=256):
    B, S, _ = z3.shape
    wq = _WIDE_OFFSETS["fox_q"] // FOX_WIDTH
    wk = _WIDE_OFFSETS["fox_k"] // FOX_WIDTH
    wv = _WIDE_OFFSETS["fox_v"] // FOX_WIDTH
    pk, pq, ck, cq = _fox_place_matrices()
    const = lambda a: pl.BlockSpec(a.shape, lambda b, i: (0,) * a.ndim)
    return pl.pallas_call(
        functools.partial(_fox_kernel, blk=blk),
        grid=(B, S // blk),
        in_specs=[
            pl.BlockSpec((None, blk, FOX_WIDTH), lambda b, i: (b, i, wq)),
            pl.BlockSpec((None, S, FOX_WIDTH), lambda b, i: (b, 0, wk)),
            pl.BlockSpec((None, S, FOX_WIDTH), lambda b, i: (b, 0, wv)),
            pl.BlockSpec((None, S, LANES), lambda b, i: (b, 0, 0)),
            const(pk), const(pq), const(ck), const(cq),
        ],
        out_specs=pl.BlockSpec((None, blk, FOX_WIDTH), lambda b, i: (b, i, 0)),
        out_shape=jax.ShapeDtypeStruct((B, S, FOX_WIDTH), BF16),
        scratch_shapes=[
            pltpu.VMEM((S, FOX_PAIRS * FOX_AUG), BF16),
            pltpu.VMEM((S // blk, FOX_HEADS, FOX_VT_ROWS, blk), BF16),
            pltpu.VMEM((FOX_PAIRS, FOX_PAIR + FOX_AUG, 2 * blk), BF16),
            pltpu.VMEM((FOX_HEADS, blk), F32),
            pltpu.VMEM((FOX_HEADS, FOX_VT_ROWS, blk), F32),
            pltpu.VMEM((FOX_HEADS, blk), F32),
            pltpu.VMEM((2, FOX_PAIRS, blk, 2 * blk), F32),
            pltpu.VMEM((FOX_HEADS, blk, blk), BF16),
        ],
        compiler_params=_params(("parallel", "arbitrary"), 48),
        name="fox",
    )(z3, z3, z3, g3, pk, pq, ck, cq)


def _mlstm_kernel(q_ref, k_ref, v_ref, o_ref, cw_ref, gn_ref, g_ref, gt_ref, y_ref,
                  cn_ref, st_ref, halo_ref, gnb_ref, *, L):
    S = q_ref.shape[0]
    HALO = halo_ref.shape[0]
    src = lax.broadcasted_iota(jnp.int32, (L, L), 0)
    qry = lax.broadcasted_iota(jnp.int32, (L, L), 1)
    triu = src <= qry
    lane = lax.broadcasted_iota(jnp.int32, (L, LANES), 1)
    ones_col = jnp.where(lane == 0, 1.0, 0.0).astype(BF16)
    k_scale = ML_HEAD_DIM ** -0.5
    dn_t = (((1,), (1,)), ((), ()))
    dn_0 = (((0,), (0,)), ((), ()))

    cn_ref[...] = jnp.zeros_like(cn_ref)
    st_ref[...] = jnp.zeros_like(st_ref)
    halo_ref[...] = jnp.zeros_like(halo_ref)
    for h in range(ML_HEADS):
        gnb_ref[h] = jnp.broadcast_to(gn_ref[h * ML_HEAD_DIM:(h + 1) * ML_HEAD_DIM, :], (ML_HEAD_DIM, L))

    def conv_silu(x_chunk, halo, w):
        xx = jnp.concatenate([halo, x_chunk], axis=0)
        y = w[CONV_WIDTH - 1:CONV_WIDTH, :] * x_chunk
        for j in range(CONV_WIDTH - 1):
            sh = CONV_WIDTH - 1 - j
            y = y + w[j:j + 1, :] * xx[HALO - sh:HALO - sh + L, :]
        return y * jax.nn.sigmoid(y)

    def chunk(ci, _):
        rows = pl.ds(pl.multiple_of(ci * L, L), L)
        xq = q_ref[rows, :].astype(F32)
        xk = k_ref[rows, :].astype(F32)
        qa = conv_silu(xq, halo_ref[:, :ML_WIDTH], cw_ref[:, :ML_WIDTH])
        ka = conv_silu(xk, halo_ref[:, ML_WIDTH:], cw_ref[:, ML_WIDTH:]) * k_scale
        halo_ref[:, :ML_WIDTH] = xq[L - HALO:, :]
        halo_ref[:, ML_WIDTH:] = xk[L - HALO:, :]
        gcol = g_ref[rows, :]
        grow = gt_ref[ci]

        for h in range(ML_HEADS):
            cols = slice(h * ML_HEAD_DIM, (h + 1) * ML_HEAD_DIM)
            qb = qa[:, cols].astype(BF16)
            kf = ka[:, cols]
            vb = v_ref[rows, cols]
            c_col = gcol[:, LANE_ML_I + h:LANE_ML_I + h + 1] - gcol[:, LANE_ML_F + h:LANE_ML_F + h + 1]
            f_r = grow[LANE_ML_F + h:LANE_ML_F + h + 1, :]
            f_prev = st_ref[h, 0:1, 0:1]
            m_prev = st_ref[h, 0:1, 1:2]

            d_t = jnp.where(triu, c_col + f_r, -jnp.inf)
            inter = f_r - f_prev + m_prev
            m = jnp.maximum(inter, jnp.max(d_t, axis=0, keepdims=True))
            w_inter = jnp.exp(inter - m)
            p_t = lax.dot_general(kf.astype(BF16), qb, dn_t, preferred_element_type=F32) * jnp.exp(d_t - m)
            io_t = lax.dot_general(cn_ref[h].astype(BF16), qb, dn_t, preferred_element_type=F32)
            pv_t = lax.dot_general(vb, p_t.astype(BF16), dn_0, preferred_element_type=F32)
            num = w_inter * io_t[:ML_HEAD_DIM, :] + pv_t
            den = w_inter * io_t[ML_HEAD_DIM:ML_HEAD_DIM + 1, :] + jnp.sum(p_t, axis=0, keepdims=True)
            h_t = num / jnp.maximum(jnp.abs(den), jnp.exp(-m))

            m_new = m[:, L - 1:L]
            f_end = f_r[:, L - 1:L]
            decay = jnp.exp(f_end - f_prev + m_prev - m_new)
            w_s = jnp.exp(jnp.broadcast_to(c_col, (L, ML_HEAD_DIM)) + (f_end - m_new))
            kw = (w_s * kf).astype(BF16)
            v_aug = jnp.concatenate([vb, ones_col], axis=-1)
            upd = lax.dot_general(v_aug, kw, dn_0, preferred_element_type=F32)
            cn_ref[h] = decay * cn_ref[h] + upd
            st_ref[h, 0:1, 0:1] = f_end
            st_ref[h, 0:1, 1:2] = m_new

            hn_t = h_t * lax.rsqrt(jnp.mean(h_t * h_t, axis=0, keepdims=True) + EPS) * gnb_ref[h]
            y = jax.nn.sigmoid(o_ref[rows, cols].astype(F32)) * hn_t.T
            y_ref[rows, cols] = y.astype(y_ref.dtype)
        return 0

    lax.fori_loop(0, S // L, chunk, 0)


def mlstm(z3, conv_w, mlstm_g, g3, gt4, *, layer, L=CHUNK):
    B, S, _ = z3.shape
    blocks = [_WIDE_OFFSETS[n] // ML_WIDTH for n in ("ml_q", "ml_k", "ml_v", "ml_o")]
    seq_spec = lambda idx: pl.BlockSpec((None, S, ML_WIDTH), lambda b: (b, 0, idx))
    return pl.pallas_call(
        functools.partial(_mlstm_kernel, L=L),
        grid=(B,),
        in_specs=[
            seq_spec(blocks[0]), seq_spec(blocks[1]), seq_spec(blocks[2]), seq_spec(blocks[3]),
            _per_layer((CONV_WIDTH, 2 * ML_WIDTH), layer),
            _per_layer((ML_WIDTH, 1), layer),
            pl.BlockSpec((None, S, LANES), lambda b: (b, 0, 0)),
            pl.BlockSpec((None, S // L, N_GATE_LANES, L), lambda b: (b, 0, 0, 0)),
        ],
        out_specs=pl.BlockSpec((None, S, ML_WIDTH), lambda b: (b, 0, 0)),
        out_shape=jax.ShapeDtypeStruct((B, S, ML_WIDTH), BF16),
        scratch_shapes=[
            pltpu.VMEM((ML_HEADS, 2 * ML_HEAD_DIM, ML_HEAD_DIM), F32),
            pltpu.VMEM((ML_HEADS, 8, LANES), F32),
            pltpu.VMEM((ML_HALO, 2 * ML_WIDTH), F32),
            pltpu.VMEM((ML_HEADS, ML_HEAD_DIM, L), F32),
        ],
        compiler_params=_params(("parallel",), 40),
        name="mlstm",
    )(z3, z3, z3, z3, conv_w, mlstm_g, g3, gt4)


def _gmlp_kernel(u_ref, v_ref, gn_ref, ws_ref, bst_ref, y_ref):
    rows = u_ref.shape[0]
    u = jax.nn.gelu(u_ref[...].astype(F32))
    v = jax.nn.gelu(v_ref[...].astype(F32))
    mu = jnp.mean(v, axis=-1, keepdims=True)
    vc = v - mu
    vn = (vc * lax.rsqrt(jnp.mean(vc * vc, axis=-1, keepdims=True) + EPS) * gn_ref[...]).astype(BF16)
    r = lax.broadcasted_iota(jnp.int32, (G_SPAN, G_SPAN), 0) // CHUNK
    c = lax.broadcasted_iota(jnp.int32, (G_SPAN, G_SPAN), 1) // CHUNK
    mask = r >= c
    for g in range(G_GROUPS):
        cols = slice(g * G_GROUP_DIM, (g + 1) * G_GROUP_DIM)
        w = jnp.where(mask, ws_ref[g], 0.0).astype(BF16)
        bias = bst_ref[:, g:g + 1]
        for s in range(rows // G_SPAN):
            rs = slice(s * G_SPAN, (s + 1) * G_SPAN)
            mixed = jnp.dot(w, vn[rs, cols], preferred_element_type=F32) + bias
            y_ref[rs, cols] = (u[rs, cols] * mixed).astype(y_ref.dtype)


def gmlp(z3, gmlp_g, ws, bs_t, *, layer, rows=512):
    B, S, _ = z3.shape
    rows = min(rows, S)
    bu = _WIDE_OFFSETS["g_u"] // G_WIDTH
    bv = _WIDE_OFFSETS["g_v"] // G_WIDTH
    return pl.pallas_call(
        _gmlp_kernel,
        grid=(B, S // rows),
        in_specs=[
            pl.BlockSpec((None, rows, G_WIDTH), lambda b, i: (b, i, bu)),
            pl.BlockSpec((None, rows, G_WIDTH), lambda b, i: (b, i, bv)),
            _per_layer((1, G_WIDTH), layer),
            _per_layer((G_GROUPS, G_SPAN, G_SPAN), layer),
            _per_layer((G_SPAN, G_GROUPS), layer),
        ],
        out_specs=pl.BlockSpec((None, rows, G_WIDTH), lambda b, i: (b, i, 0)),
        out_shape=jax.ShapeDtypeStruct((B, S, G_WIDTH), BF16),
        compiler_params=_params(("parallel", "parallel"), 16),
        name="gmlp",
    )(z3, z3, gmlp_g, ws, bs_t)


def _merge_kernel(x_ref, gate_ref, ya_ref, yb_ref, yc_ref, wb_ref, wo_ref, gpost_ref, o_ref, *, sub):
    for r in range(x_ref.shape[0] // sub):
        rows = slice(r * sub, (r + 1) * sub)
        merged = None
        for n, y_ref in enumerate((ya_ref, yb_ref, yc_ref)):
            br = jnp.dot(y_ref[rows, :], wb_ref[n], preferred_element_type=F32)
            gt = jax.nn.sigmoid(gate_ref[rows, n * D_MODEL:(n + 1) * D_MODEL].astype(F32))
            merged = gt * br if merged is None else merged + gt * br
        y = jnp.dot(merged.astype(BF16), wo_ref[...], preferred_element_type=F32)
        o_ref[rows, :] = x_ref[rows, :] + _rms(y, gpost_ref[...])


def merge(x2, z2, y_fox, y_ml, y_g, w_branch, w_out, gains, *, layer, gain, tm=1024, sub=512):
    T, D = x2.shape
    tm = min(tm, T)
    row = lambda w: pl.BlockSpec((tm, w), lambda i: (i, 0))
    return pl.pallas_call(
        functools.partial(_merge_kernel, sub=min(sub, tm)),
        grid=(T // tm,),
        in_specs=[
            row(D),
            row(N_BRANCH * D),
            row(FOX_WIDTH), row(ML_WIDTH), row(G_WIDTH),
            _per_layer((N_BRANCH, FOX_WIDTH, D), layer),
            _per_layer((D, D), layer),
            _per_layer((1, D), gain),
        ],
        out_specs=row(D),
        out_shape=jax.ShapeDtypeStruct((T, D), F32),
        compiler_params=_params(("parallel",), 48),
        name="merge",
    )(x2, z2, y_fox, y_ml, y_g, w_branch, w_out, gains)


def _norm_matmul_kernel(x_ref, g_ref, w_ref, o_ref):
    h = _rms(x_ref[...], g_ref[...]).astype(BF16)
    o_ref[...] = jnp.dot(h, w_ref[...], preferred_element_type=F32).astype(o_ref.dtype)


def norm_matmul(x2, gains, w, *, layer, gain, tm=512):
    T, D = x2.shape
    N = w.shape[-1]
    tm = min(tm, T)
    return pl.pallas_call(
        _norm_matmul_kernel,
        grid=(T // tm,),
        in_specs=[
            pl.BlockSpec((tm, D), lambda i: (i, 0)),
            _per_layer((1, D), gain),
            _per_layer((D, N), layer),
        ],
        out_specs=pl.BlockSpec((tm, N), lambda i: (i, 0)),
        out_shape=jax.ShapeDtypeStruct((T, N), BF16),
        compiler_params=_params(("parallel",), 40),
        name="mem_kv",
    )(x2, gains, w)


def _xattn_kernel(x_ref, kv_ref, wq_ref, wo_ref, gpre_ref, gpost_ref, o_ref, *, sub):
    dn = (((1,), (1,)), ((), ()))
    for r in range(x_ref.shape[0] // sub):
        rows = slice(r * sub, (r + 1) * sub)
        x = x_ref[rows, :]
        h = _rms(x, gpre_ref[...]).astype(BF16)
        q = jnp.dot(h, wq_ref[...], preferred_element_type=F32).astype(BF16)
        outs = []
        for hd in range(X_HEADS):
            cols = slice(hd * X_HEAD_DIM, (hd + 1) * X_HEAD_DIM)
            k = kv_ref[:, cols]
            v = kv_ref[:, D_MODEL + hd * X_HEAD_DIM:D_MODEL + (hd + 1) * X_HEAD_DIM]
            s = lax.dot_general(q[:, cols], k, dn, preferred_element_type=F32) * (X_HEAD_DIM ** -0.5)
            s = s - jnp.max(s, axis=-1, keepdims=True)
            e = jnp.exp(s)
            p = e / jnp.sum(e, axis=-1, keepdims=True)
            outs.append(jnp.dot(p.astype(BF16), v, preferred_element_type=F32).astype(BF16))
        o = jnp.concatenate(outs, axis=-1)
        y = jnp.dot(o, wo_ref[...], preferred_element_type=F32)
        o_ref[rows, :] = x + _rms(y, gpost_ref[...])


def xattn(x3, kv3, w_q, w_o, gains, *, layer, gain_pre, gain_post, tm=1024, sub=512):
    B, S, D = x3.shape
    M = kv3.shape[1]
    tm = min(tm, S)
    return pl.pallas_call(
        functools.partial(_xattn_kernel, sub=min(sub, tm)),
        grid=(B, S // tm),
        in_specs=[
            pl.BlockSpec((None, tm, D), lambda b, i: (b, i, 0)),
            pl.BlockSpec((None, M, 2 * D), lambda b, i: (b, 0, 0)),
            _per_layer((D, D), layer), _per_layer((D, D), layer),
            _per_layer((1, D), gain_pre), _per_layer((1, D), gain_post),
        ],
        out_specs=pl.BlockSpec((None, tm, D), lambda b, i: (b, i, 0)),
        out_shape=jax.ShapeDtypeStruct((B, S, D), F32),
        compiler_params=_params(("parallel", "arbitrary"), 48),
        name="xattn",
    )(x3, kv3, w_q, w_o, gains, gains)


def _ffn_kernel(x_ref, w1_ref, w2_ref, gpre_ref, gpost_ref, o_ref, *, ff_chunk, sub):
    for r in range(x_ref.shape[0] // sub):
        rows = slice(r * sub, (r + 1) * sub)
        x = x_ref[rows, :]
        h = _rms(x, gpre_ref[...]).astype(BF16)
        acc = None
        for c in range(w1_ref.shape[1] // ff_chunk):
            cs = slice(c * ff_chunk, (c + 1) * ff_chunk)
            a = jnp.dot(h, w1_ref[:, cs], preferred_element_type=F32)
            a = jnp.square(jnp.maximum(a, 0.0)).astype(BF16)
            part = jnp.dot(a, w2_ref[cs, :], preferred_element_type=F32)
            acc = part if acc is None else acc + part
        o_ref[rows, :] = x + _rms(acc, gpost_ref[...])


def ffn(x2, w1, w2, gains, *, layer, gain_pre, gain_post, tm=1024, sub=512, ff_chunk=1024):
    T, D = x2.shape
    FF = w1.shape[-1]
    tm = min(tm, T)
    return pl.pallas_call(
        functools.partial(_ffn_kernel, ff_chunk=ff_chunk, sub=min(sub, tm)),
        grid=(T // tm,),
        in_specs=[
            pl.BlockSpec((tm, D), lambda i: (i, 0)),
            _per_layer((D, FF), layer),
            _per_layer((FF, D), layer),
            _per_layer((1, D), gain_pre),
            _per_layer((1, D), gain_post),
        ],
        out_specs=pl.BlockSpec((tm, D), lambda i: (i, 0)),
        out_shape=jax.ShapeDtypeStruct((T, D), F32),
        compiler_params=_params(("parallel",), 56),
        name="ffn",
    )(x2, w1, w2, gains, gains)


def _regroup_in_proj(w_in, b_in):
    def cols(names):
        return [slice(_IN_OFFSETS[n][0], _IN_OFFSETS[n][0] + _IN_OFFSETS[n][1]) for n in names]
    w_wide = jnp.concatenate([w_in[..., s] for s in cols(_WIDE_ORDER)], axis=-1).astype(BF16)
    b_wide = jnp.concatenate([b_in[..., s] for s in cols(_WIDE_ORDER)], axis=-1)[:, None, :]
    pad = LANES - N_GATE_LANES
    w_narrow = jnp.pad(jnp.concatenate([w_in[..., s] for s in cols(_NARROW_ORDER)], axis=-1),
                       ((0, 0), (0, 0), (0, pad))).astype(BF16)
    b_narrow = jnp.pad(jnp.concatenate([b_in[..., s] for s in cols(_NARROW_ORDER)], axis=-1),
                       ((0, 0), (0, pad)))[:, None, :]
    return w_wide, b_wide, w_narrow, b_narrow


def _time_on_lanes(g3, blk):
    B, S, _ = g3.shape
    return g3[:, :, :N_GATE_LANES].reshape(B, S // blk, blk, N_GATE_LANES).transpose(0, 1, 3, 2)


def _prepare_params(norms, w_in, b_in, conv_w, mlstm_norm, gmlp_norm, gmlp_ws, gmlp_bs,
                    w_branch, w_out, w_xq, w_xkv, w_xo, w_ff1, w_ff2):
    depth = norms.shape[0]
    w_wide, b_wide, w_narrow, b_narrow = _regroup_in_proj(w_in, b_in)
    return dict(
        gains=norms.reshape(depth * N_NORMS, 1, D_MODEL),
        w_wide=w_wide, b_wide=b_wide, w_narrow=w_narrow, b_narrow=b_narrow,
        conv_w=conv_w, mlstm_g=mlstm_norm[:, :, None], gmlp_g=gmlp_norm[:, None, :],
        gmlp_ws=gmlp_ws, gmlp_bs_t=gmlp_bs.transpose(0, 2, 1),
        w_branch=w_branch.astype(BF16), w_out=w_out.astype(BF16),
        w_xq=w_xq.astype(BF16), w_xkv=w_xkv.astype(BF16), w_xo=w_xo.astype(BF16),
        w_ff1=w_ff1.astype(BF16), w_ff2=w_ff2.astype(BF16),
    )


def _layer(x3, mem2, p, layer, *, fox_blk, ml_chunk):
    B, S, D = x3.shape
    T = B * S
    M = mem2.shape[0] // B
    gain = lambda idx: layer * N_NORMS + idx
    x2 = x3.reshape(T, D)

    z2, zn2 = in_proj(x2, p["gains"], p["w_wide"], p["b_wide"], p["w_narrow"], p["b_narrow"],
                      layer=layer, gain=gain(NORM_MIX_PRE))
    z3 = z2.reshape(B, S, N_WIDE)
    g3 = gates(zn2.reshape(B, S, LANES))

    y_fox = fox_attention(z3, g3, blk=fox_blk)
    y_ml = mlstm(z3, p["conv_w"], p["mlstm_g"], g3, _time_on_lanes(g3, ml_chunk), layer=layer, L=ml_chunk)
    y_g = gmlp(z3, p["gmlp_g"], p["gmlp_ws"], p["gmlp_bs_t"], layer=layer)
    x2 = merge(x2, z2, y_fox.reshape(T, FOX_WIDTH), y_ml.reshape(T, ML_WIDTH), y_g.reshape(T, G_WIDTH),
               p["w_branch"], p["w_out"], p["gains"], layer=layer, gain=gain(NORM_MIX_POST))

    kv = norm_matmul(mem2, p["gains"], p["w_xkv"], layer=layer, gain=gain(NORM_MEM))
    x3 = xattn(x2.reshape(B, S, D), kv.reshape(B, M, 2 * D), p["w_xq"], p["w_xo"], p["gains"],
               layer=layer, gain_pre=gain(NORM_X_PRE), gain_post=gain(NORM_X_POST))

    x2 = ffn(x3.reshape(T, D), p["w_ff1"], p["w_ff2"], p["gains"],
             layer=layer, gain_pre=gain(NORM_FF_PRE), gain_post=gain(NORM_FF_POST))
    return x2.reshape(B, S, D)


def kernel(x, mem, norms, w_in, b_in, conv_w, mlstm_norm, gmlp_norm, gmlp_ws, gmlp_bs, w_branch, w_out, w_xq, w_xkv, w_xo, w_ff1, w_ff2):
    B, M, D = mem.shape
    S = x.shape[1]
    mem2 = mem.reshape(B * M, D)
    p = _prepare_params(norms, w_in, b_in, conv_w, mlstm_norm, gmlp_norm, gmlp_ws, gmlp_bs,
                        w_branch, w_out, w_xq, w_xkv, w_xo, w_ff1, w_ff2)
    for layer in range(norms.shape[0]):
        x = _layer(x, mem2, p, layer, fox_blk=min(256, S), ml_chunk=min(ML_CHUNK, S))
    return x
```

```python
import functools

import jax
import jax.numpy as jnp
import numpy as np
from jax import lax
from jax.experimental import pallas as pl
from jax.experimental.pallas import tpu as pltpu

F32 = jnp.float32
BF16 = jnp.bfloat16

EPS = 1e-6
D_MODEL = 1024
CHUNK = 64
FOX_HEADS = 8
FOX_HEAD_DIM = 64
FOX_WIDTH = FOX_HEADS * FOX_HEAD_DIM
ML_HEADS = 4
ML_HEAD_DIM = 128
ML_WIDTH = ML_HEADS * ML_HEAD_DIM
ML_CHUNK = 128
ML_HALO = 8
CONV_WIDTH = 4
G_GROUPS = 4
G_GROUP_DIM = 128
G_WIDTH = G_GROUPS * G_GROUP_DIM
G_SPAN = 128
N_BRANCH = 3
X_HEADS = 4
X_HEAD_DIM = D_MODEL // X_HEADS
D_FF = 4 * D_MODEL
NORM_MIX_PRE, NORM_MIX_POST, NORM_X_PRE, NORM_X_POST, NORM_MEM, NORM_FF_PRE, NORM_FF_POST = range(7)
N_NORMS = 7

LOG2E = 1.4426950408889634
LANES = 128
MIB = 1024 * 1024

_IN_SPLITS = (
    ("fox_q", FOX_WIDTH), ("fox_k", FOX_WIDTH), ("fox_v", FOX_WIDTH), ("fox_f", FOX_HEADS),
    ("ml_q", ML_WIDTH), ("ml_k", ML_WIDTH), ("ml_v", ML_WIDTH),
    ("ml_i", ML_HEADS), ("ml_f", ML_HEADS), ("ml_o", ML_WIDTH),
    ("g_u", G_WIDTH), ("g_v", G_WIDTH),
    ("gate", N_BRANCH * D_MODEL),
)
_IN_OFFSETS = {}
_off = 0
for _name, _size in _IN_SPLITS:
    _IN_OFFSETS[_name] = (_off, _size)
    _off += _size

_WIDE_ORDER = ("gate", "fox_q", "fox_k", "fox_v", "ml_q", "ml_k", "ml_v", "ml_o", "g_u", "g_v")
_WIDE_OFFSETS = {}
_off = 0
for _name in _WIDE_ORDER:
    _WIDE_OFFSETS[_name] = _off
    _off += _IN_OFFSETS[_name][1]
N_WIDE = _off
_NARROW_ORDER = ("fox_f", "ml_i", "ml_f")
LANE_FOX_F = 0
LANE_ML_I = FOX_HEADS
LANE_ML_F = FOX_HEADS + ML_HEADS
N_GATE_LANES = FOX_HEADS + 2 * ML_HEADS


def _rms(x, g):
    return x * lax.rsqrt(jnp.mean(x * x, axis=-1, keepdims=True) + EPS) * g


def _log_sigmoid(x):
    return jnp.minimum(x, 0.0) - jnp.log1p(jnp.exp(-jnp.abs(x)))


def _params(semantics, vmem_mib):
    return pltpu.CompilerParams(dimension_semantics=semantics, vmem_limit_bytes=vmem_mib * MIB)


def _per_layer(block, layer, tail=None):
    block = tuple(block)
    if tail is None:
        return pl.BlockSpec((None,) + block, lambda *ids: (layer,) + (0,) * len(block),
                            pipeline_mode=pl.Buffered(1))
    return pl.BlockSpec((None,) + block, lambda *ids: (layer,) + tuple(tail(*ids)))


def _in_proj_kernel(x_ref, g_ref, w_ref, b_ref, wn_ref, bn_ref, z_ref, zn_ref, h_ref, *, sub):
    j = pl.program_id(1)

    @pl.when(j == 0)
    def _():
        for r in range(x_ref.shape[0] // sub):
            rows = slice(r * sub, (r + 1) * sub)
            h = _rms(x_ref[rows, :], g_ref[...]).astype(BF16)
            h_ref[rows, :] = h
            zn_ref[rows, :] = jnp.dot(h, wn_ref[...], preferred_element_type=F32) + bn_ref[...]
            z = jnp.dot(h, w_ref[...], preferred_element_type=F32) + b_ref[...]
            z_ref[rows, :] = z.astype(z_ref.dtype)

    @pl.when(j != 0)
    def _():
        z = jnp.dot(h_ref[...], w_ref[...], preferred_element_type=F32) + b_ref[...]
        z_ref[...] = z.astype(z_ref.dtype)


def in_proj(x2, gains, w_wide, b_wide, w_narrow, b_narrow, *, layer, gain, tm=2048, tn=1536, sub=512):
    T, D = x2.shape
    N = w_wide.shape[-1]
    tm = min(tm, T)
    return pl.pallas_call(
        functools.partial(_in_proj_kernel, sub=min(sub, tm)),
        grid=(T // tm, N // tn),
        in_specs=[
            pl.BlockSpec((tm, D), lambda i, j: (i, 0)),
            _per_layer((1, D), gain),
            _per_layer((D, tn), layer, lambda i, j: (0, j)),
            _per_layer((1, tn), layer, lambda i, j: (0, j)),
            _per_layer((D, LANES), layer),
            _per_layer((1, LANES), layer),
        ],
        out_specs=[
            pl.BlockSpec((tm, tn), lambda i, j: (i, j)),
            pl.BlockSpec((tm, LANES), lambda i, j: (i, 0)),
        ],
        out_shape=[
            jax.ShapeDtypeStruct((T, N), BF16),
            jax.ShapeDtypeStruct((T, LANES), F32),
        ],
        scratch_shapes=[pltpu.VMEM((tm, D), BF16)],
        compiler_params=_params(("parallel", "arbitrary"), 56),
        name="in_proj",
    )(x2, gains, w_wide, b_wide, w_narrow, b_narrow)


def _gates_kernel(zn_ref, g_ref, gt_ref, *, rows):
    S = zn_ref.shape[0]
    L = gt_ref.shape[-1]
    lane = lax.broadcasted_iota(jnp.int32, (rows, LANES), 1)
    keep_raw = jnp.logical_and(lane >= LANE_ML_I, lane < LANE_ML_F)
    r = lax.broadcasted_iota(jnp.int32, (rows, rows), 0)
    c = lax.broadcasted_iota(jnp.int32, (rows, rows), 1)
    tril = jnp.where(r >= c, 1.0, 0.0).astype(F32)

    offset = jnp.zeros((1, LANES), F32)
    for i in range(S // rows):
        sl = slice(i * rows, (i + 1) * rows)
        z = zn_ref[sl, :]
        ls = _log_sigmoid(z)
        cs = jnp.dot(tril, ls, precision=lax.Precision.HIGHEST, preferred_element_type=F32) + offset
        out = jnp.where(keep_raw, z, cs)
        g_ref[sl, :] = out
        out_t = out.T
        for k in range(rows // L):
            gt_ref[i * (rows // L) + k] = out_t[:N_GATE_LANES, k * L:(k + 1) * L]
        offset = cs[rows - 1:rows, :]


def gates(zn3, *, L, rows=256):
    B, S, _ = zn3.shape
    rows = min(rows, S)
    assert rows % L == 0
    return pl.pallas_call(
        functools.partial(_gates_kernel, rows=rows),
        grid=(B,),
        in_specs=[pl.BlockSpec((None, S, LANES), lambda b: (b, 0, 0))],
        out_specs=[pl.BlockSpec((None, S, LANES), lambda b: (b, 0, 0)),
                   pl.BlockSpec((None, S // L, N_GATE_LANES, L), lambda b: (b, 0, 0, 0))],
        out_shape=[jax.ShapeDtypeStruct((B, S, LANES), F32),
                   jax.ShapeDtypeStruct((B, S // L, N_GATE_LANES, L), F32)],
        compiler_params=_params(("parallel",), 16),
        name="gates",
    )(zn3)


FOX_AUG = LANES
FOX_SUB = 8
FOX_PAIR = 2 * FOX_HEAD_DIM
FOX_PAIRS = FOX_HEADS // 2
FOX_VT_ROWS = FOX_HEAD_DIM + 16


def _fox_place_matrices():
    pk = np.zeros((3 * LANES, FOX_PAIRS * FOX_AUG), np.float32)
    pq = np.zeros((3 * LANES, FOX_PAIRS * FOX_AUG), np.float32)
    ck = np.zeros((1, FOX_PAIRS * FOX_AUG), np.float32)
    cq = np.zeros((1, FOX_PAIRS * FOX_AUG), np.float32)
    for h in range(FOX_HEADS):
        base = (h // 2) * FOX_AUG + (h % 2) * FOX_SUB
        for piece in range(3):
            pk[piece * LANES + LANE_FOX_F + h, base + piece] = -1.0
            pq[piece * LANES + LANE_FOX_F + h, base + 3 + piece] = 1.0
            ck[0, base + 3 + piece] = 1.0
            cq[0, base + piece] = 1.0
    return (jnp.asarray(pk, BF16), jnp.asarray(pq, BF16), jnp.asarray(ck), jnp.asarray(cq))


def _fox_bias_lanes(f, place_ref, ones_ref):
    f = f * LOG2E
    hi = f.astype(BF16)
    r1 = f - hi.astype(F32)
    mid = r1.astype(BF16)
    lo = (r1 - mid.astype(F32)).astype(BF16)
    x = jnp.concatenate([hi, mid, lo], axis=-1)
    return (jnp.dot(x, place_ref[...], preferred_element_type=F32) + ones_ref[...]).astype(BF16)


def _fox_kernel(q_ref, k_ref, v_ref, g_ref, pk_ref, pq_ref, ck_ref, cq_ref, o_ref,
                kaug_ref, vt_ref, qaug_ref, m_ref, acc_ref, al_ref, st_ref, p_ref, *, blk):
    S = k_ref.shape[0]
    qi = pl.program_id(1)
    key_pos = lax.broadcasted_iota(jnp.int32, (blk, blk), 0)
    qry_pos = lax.broadcasted_iota(jnp.int32, (blk, blk), 1)
    causal = key_pos <= qry_pos

    @pl.when(qi == 0)
    def _():
        ones = jnp.ones((FOX_VT_ROWS - FOX_HEAD_DIM, blk), BF16)
        for c in range(S // blk):
            rows = slice(c * blk, (c + 1) * blk)
            kaug_ref[rows, :] = _fox_bias_lanes(g_ref[rows, :], pk_ref, ck_ref)
            for j in range(FOX_PAIRS):
                vt = v_ref[rows, j * FOX_PAIR:(j + 1) * FOX_PAIR].T
                for hh in range(2):
                    vt_ref[c, 2 * j + hh, :FOX_HEAD_DIM, :] = vt[hh * FOX_HEAD_DIM:(hh + 1) * FOX_HEAD_DIM, :]
                    vt_ref[c, 2 * j + hh, FOX_HEAD_DIM:, :] = ones

    q_rows = pl.ds(pl.multiple_of(qi * blk, blk), blk)
    qaug = _fox_bias_lanes(g_ref[q_rows, :], pq_ref, cq_ref)
    q_half = lax.broadcasted_iota(jnp.int32, (blk, FOX_PAIR), 1) // FOX_HEAD_DIM
    aug_half = lax.broadcasted_iota(jnp.int32, (blk, FOX_AUG), 1) // FOX_SUB
    zero = jnp.zeros((), BF16)
    scale = FOX_HEAD_DIM ** -0.5 * LOG2E
    for j in range(FOX_PAIRS):
        qs = (q_ref[:, j * FOX_PAIR:(j + 1) * FOX_PAIR].astype(F32) * scale).astype(BF16)
        qa = qaug[:, j * FOX_AUG:(j + 1) * FOX_AUG]
        both = [jnp.concatenate([jnp.where(q_half == hh, qs, zero), jnp.where(aug_half == hh, qa, zero)], axis=-1).T
                for hh in range(2)]
        qaug_ref[j] = jnp.concatenate(both, axis=-1)
    m_ref[...] = jnp.full_like(m_ref, -jnp.inf)
    acc_ref[...] = jnp.zeros_like(acc_ref)

    def score_pair(kb, slot, j):
        ks = pl.ds(pl.multiple_of(kb * blk, blk), blk)
        kk = jnp.concatenate([k_ref[ks, j * FOX_PAIR:(j + 1) * FOX_PAIR],
                              kaug_ref[ks, j * FOX_AUG:(j + 1) * FOX_AUG]], axis=-1)
        st_ref[slot, j] = jnp.dot(kk, qaug_ref[j], preferred_element_type=F32)

    def softmax_head(slot, h, masked):
        st = st_ref[slot, h // 2, :, (h % 2) * blk:(h % 2 + 1) * blk]
        if masked:
            st = jnp.where(causal, st, -jnp.inf)
        m_old = m_ref[h:h + 1, :]
        m_new = jnp.maximum(m_old, jnp.max(st, axis=0, keepdims=True))
        al_ref[h:h + 1, :] = jnp.exp2(m_old - m_new)
        m_ref[h:h + 1, :] = m_new
        p_ref[h] = jnp.exp2(st - m_new).astype(BF16)

    def value_head(kb, h):
        pv = jnp.dot(vt_ref[kb, h], p_ref[h], preferred_element_type=F32)
        acc_ref[h] = al_ref[h:h + 1, :] * acc_ref[h] + pv

    def advance(kb, slot, masked, prefetch):
        for j in range(FOX_PAIRS):
            if prefetch:
                score_pair(kb + 1, 1 - slot, j)
            softmax_head(slot, 2 * j, masked)
            softmax_head(slot, 2 * j + 1, masked)
            if j > 0:
                value_head(kb, 2 * j - 2)
                value_head(kb, 2 * j - 1)
        value_head(kb, FOX_HEADS - 2)
        value_head(kb, FOX_HEADS - 1)

    for j in range(FOX_PAIRS):
        score_pair(0, 0, j)

    def two_blocks(i, carry):
        advance(2 * i, 0, False, True)
        advance(2 * i + 1, 1, False, True)
        return carry

    lax.fori_loop(0, qi // 2, two_blocks, 0)

    @pl.when(qi % 2 == 0)
    def _():
        advance(qi, 0, True, False)

    @pl.when(qi % 2 == 1)
    def _():
        advance(qi - 1, 0, False, True)
        advance(qi, 1, True, False)

    outs = []
    for j in range(FOX_HEADS // 2):
        tops = []
        for h in (2 * j, 2 * j + 1):
            a = acc_ref[h]
            tops.append(a[:FOX_HEAD_DIM, :] / a[FOX_HEAD_DIM:FOX_HEAD_DIM + 1, :])
        outs.append(jnp.concatenate(tops, axis=0).T)
    o_ref[...] = jnp.concatenate(outs, axis=-1).astype(o_ref.dtype)


def fox_attention(z3, g3, *, blk=256):
    B, S, _ = z3.shape
    wq = _WIDE_OFFSETS["fox_q"] // FOX_WIDTH
    wk = _WIDE_OFFSETS["fox_k"] // FOX_WIDTH
    wv = _WIDE_OFFSETS["fox_v"] // FOX_WIDTH
    pk, pq, ck, cq = _fox_place_matrices()
    const = lambda a: pl.BlockSpec(a.shape, lambda b, i: (0,) * a.ndim)
    return pl.pallas_call(
        functools.partial(_fox_kernel, blk=blk),
        grid=(B, S // blk),
        in_specs=[
            pl.BlockSpec((None, blk, FOX_WIDTH), lambda b, i: (b, i, wq)),
            pl.BlockSpec((None, S, FOX_WIDTH), lambda b, i: (b, 0, wk)),
            pl.BlockSpec((None, S, FOX_WIDTH), lambda b, i: (b, 0, wv)),
            pl.BlockSpec((None, S, LANES), lambda b, i: (b, 0, 0)),
            const(pk), const(pq), const(ck), const(cq),
        ],
        out_specs=pl.BlockSpec((None, blk, FOX_WIDTH), lambda b, i: (b, i, 0)),
        out_shape=jax.ShapeDtypeStruct((B, S, FOX_WIDTH), BF16),
        scratch_shapes=[
            pltpu.VMEM((S, FOX_PAIRS * FOX_AUG), BF16),
            pltpu.VMEM((S // blk, FOX_HEADS, FOX_VT_ROWS, blk), BF16),
            pltpu.VMEM((FOX_PAIRS, FOX_PAIR + FOX_AUG, 2 * blk), BF16),
            pltpu.VMEM((FOX_HEADS, blk), F32),
            pltpu.VMEM((FOX_HEADS, FOX_VT_ROWS, blk), F32),
            pltpu.VMEM((FOX_HEADS, blk), F32),
            pltpu.VMEM((2, FOX_PAIRS, blk, 2 * blk), F32),
            pltpu.VMEM((FOX_HEADS, blk, blk), BF16),
        ],
        compiler_params=_params(("parallel", "arbitrary"), 48),
        name="fox",
    )(z3, z3, z3, g3, pk, pq, ck, cq)


def _mlstm_kernel(q_ref, k_ref, v_ref, o_ref, cw_ref, gn_ref, g_ref, gt_ref, y_ref,
                  cn_ref, st_ref, halo_ref, gnb_ref, *, L):
    S = q_ref.shape[0]
    HALO = halo_ref.shape[0]
    src = lax.broadcasted_iota(jnp.int32, (L, L), 0)
    qry = lax.broadcasted_iota(jnp.int32, (L, L), 1)
    triu = src <= qry
    lane = lax.broadcasted_iota(jnp.int32, (L, LANES), 1)
    ones_col = jnp.where(lane == 0, 1.0, 0.0).astype(BF16)
    k_scale = ML_HEAD_DIM ** -0.5
    dn_t = (((1,), (1,)), ((), ()))
    dn_0 = (((0,), (0,)), ((), ()))

    cn_ref[...] = jnp.zeros_like(cn_ref)
    st_ref[...] = jnp.zeros_like(st_ref)
    halo_ref[...] = jnp.zeros_like(halo_ref)
    for h in range(ML_HEADS):
        gnb_ref[h] = jnp.broadcast_to(gn_ref[h * ML_HEAD_DIM:(h + 1) * ML_HEAD_DIM, :], (ML_HEAD_DIM, L))

    def conv_silu(x_chunk, halo, w):
        xx = jnp.concatenate([halo, x_chunk], axis=0)
        y = w[CONV_WIDTH - 1:CONV_WIDTH, :] * x_chunk
        for j in range(CONV_WIDTH - 1):
            sh = CONV_WIDTH - 1 - j
            y = y + w[j:j + 1, :] * xx[HALO - sh:HALO - sh + L, :]
        return y * jax.nn.sigmoid(y)

    def chunk(ci, _):
        rows = pl.ds(pl.multiple_of(ci * L, L), L)
        xq = q_ref[rows, :].astype(F32)
        xk = k_ref[rows, :].astype(F32)
        qa = conv_silu(xq, halo_ref[:, :ML_WIDTH], cw_ref[:, :ML_WIDTH])
        ka = conv_silu(xk, halo_ref[:, ML_WIDTH:], cw_ref[:, ML_WIDTH:]) * k_scale
        halo_ref[:, :ML_WIDTH] = xq[L - HALO:, :]
        halo_ref[:, ML_WIDTH:] = xk[L - HALO:, :]
        gcol = g_ref[rows, :]
        grow = gt_ref[ci]

        for h in range(ML_HEADS):
            cols = slice(h * ML_HEAD_DIM, (h + 1) * ML_HEAD_DIM)
            qb = qa[:, cols].astype(BF16)
            kf = ka[:, cols]
            vb = v_ref[rows, cols]
            c_col = gcol[:, LANE_ML_I + h:LANE_ML_I + h + 1] - gcol[:, LANE_ML_F + h:LANE_ML_F + h + 1]
            f_r = grow[LANE_ML_F + h:LANE_ML_F + h + 1, :]
            f_prev = st_ref[h, 0:1, 0:1]
            m_prev = st_ref[h, 0:1, 1:2]

            d_t = jnp.where(triu, c_col + f_r, -jnp.inf)
            inter = f_r - f_prev + m_prev
            m = jnp.maximum(inter, jnp.max(d_t, axis=0, keepdims=True))
            w_inter = jnp.exp(inter - m)
            p_t = lax.dot_general(kf.astype(BF16), qb, dn_t, preferred_element_type=F32) * jnp.exp(d_t - m)
            io_t = lax.dot_general(cn_ref[h].astype(BF16), qb, dn_t, preferred_element_type=F32)
            pv_t = lax.dot_general(vb, p_t.astype(BF16), dn_0, preferred_element_type=F32)
            num = w_inter * io_t[:ML_HEAD_DIM, :] + pv_t
            den = w_inter * io_t[ML_HEAD_DIM:ML_HEAD_DIM + 1, :] + jnp.sum(p_t, axis=0, keepdims=True)
            h_t = num / jnp.maximum(jnp.abs(den), jnp.exp(-m))

            m_new = m[:, L - 1:L]
            f_end = f_r[:, L - 1:L]
            decay = jnp.exp(f_end - f_prev + m_prev - m_new)
            w_s = jnp.exp(jnp.broadcast_to(c_col, (L, ML_HEAD_DIM)) + (f_end - m_new))
            kw = (w_s * kf).astype(BF16)
            v_aug = jnp.concatenate([vb, ones_col], axis=-1)
            upd = lax.dot_general(v_aug, kw, dn_0, preferred_element_type=F32)
            cn_ref[h] = decay * cn_ref[h] + upd
            st_ref[h, 0:1, 0:1] = f_end
            st_ref[h, 0:1, 1:2] = m_new

            hn_t = h_t * lax.rsqrt(jnp.mean(h_t * h_t, axis=0, keepdims=True) + EPS) * gnb_ref[h]
            y = jax.nn.sigmoid(o_ref[rows, cols].astype(F32)) * hn_t.T
            y_ref[rows, cols] = y.astype(y_ref.dtype)
        return 0

    lax.fori_loop(0, S // L, chunk, 0)


def mlstm(z3, conv_w, mlstm_g, g3, gt4, *, layer, L=CHUNK):
    B, S, _ = z3.shape
    blocks = [_WIDE_OFFSETS[n] // ML_WIDTH for n in ("ml_q", "ml_k", "ml_v", "ml_o")]
    seq_spec = lambda idx: pl.BlockSpec((None, S, ML_WIDTH), lambda b: (b, 0, idx))
    return pl.pallas_call(
        functools.partial(_mlstm_kernel, L=L),
        grid=(B,),
        in_specs=[
            seq_spec(blocks[0]), seq_spec(blocks[1]), seq_spec(blocks[2]), seq_spec(blocks[3]),
            _per_layer((CONV_WIDTH, 2 * ML_WIDTH), layer),
            _per_layer((ML_WIDTH, 1), layer),
            pl.BlockSpec((None, S, LANES), lambda b: (b, 0, 0)),
            pl.BlockSpec((None, S // L, N_GATE_LANES, L), lambda b: (b, 0, 0, 0)),
        ],
        out_specs=pl.BlockSpec((None, S, ML_WIDTH), lambda b: (b, 0, 0)),
        out_shape=jax.ShapeDtypeStruct((B, S, ML_WIDTH), BF16),
        scratch_shapes=[
            pltpu.VMEM((ML_HEADS, 2 * ML_HEAD_DIM, ML_HEAD_DIM), F32),
            pltpu.VMEM((ML_HEADS, 8, LANES), F32),
            pltpu.VMEM((ML_HALO, 2 * ML_WIDTH), F32),
            pltpu.VMEM((ML_HEADS, ML_HEAD_DIM, L), F32),
        ],
        compiler_params=_params(("parallel",), 40),
        name="mlstm",
    )(z3, z3, z3, z3, conv_w, mlstm_g, g3, gt4)


def _gelu_tanh(x):
    c0 = -2.0 * np.sqrt(2.0 / np.pi) * LOG2E
    c1 = c0 * 0.044715
    return x / (1.0 + jnp.exp2(x * (c0 + c1 * (x * x))))


def _gmlp_kernel(u_ref, v_ref, gn_ref, ws_ref, bst_ref, y_ref):
    rows = u_ref.shape[0]
    u = _gelu_tanh(u_ref[...].astype(F32))
    v = _gelu_tanh(v_ref[...].astype(F32))
    mu = jnp.mean(v, axis=-1, keepdims=True)
    vc = v - mu
    vn = (vc * lax.rsqrt(jnp.mean(vc * vc, axis=-1, keepdims=True) + EPS) * gn_ref[...]).astype(BF16)
    r = lax.broadcasted_iota(jnp.int32, (G_SPAN, G_SPAN), 0) // CHUNK
    c = lax.broadcasted_iota(jnp.int32, (G_SPAN, G_SPAN), 1) // CHUNK
    mask = r >= c
    for g in range(G_GROUPS):
        cols = slice(g * G_GROUP_DIM, (g + 1) * G_GROUP_DIM)
        w = jnp.where(mask, ws_ref[g], 0.0).astype(BF16)
        bias = bst_ref[:, g:g + 1]
        for s in range(rows // G_SPAN):
            rs = slice(s * G_SPAN, (s + 1) * G_SPAN)
            mixed = jnp.dot(w, vn[rs, cols], preferred_element_type=F32) + bias
            y_ref[rs, cols] = (u[rs, cols] * mixed).astype(y_ref.dtype)


def gmlp(z3, gmlp_g, ws, bs_t, *, layer, rows=512):
    B, S, _ = z3.shape
    rows = min(rows, S)
    bu = _WIDE_OFFSETS["g_u"] // G_WIDTH
    bv = _WIDE_OFFSETS["g_v"] // G_WIDTH
    return pl.pallas_call(
        _gmlp_kernel,
        grid=(B, S // rows),
        in_specs=[
            pl.BlockSpec((None, rows, G_WIDTH), lambda b, i: (b, i, bu)),
            pl.BlockSpec((None, rows, G_WIDTH), lambda b, i: (b, i, bv)),
            _per_layer((1, G_WIDTH), layer),
            _per_layer((G_GROUPS, G_SPAN, G_SPAN), layer),
            _per_layer((G_SPAN, G_GROUPS), layer),
        ],
        out_specs=pl.BlockSpec((None, rows, G_WIDTH), lambda b, i: (b, i, 0)),
        out_shape=jax.ShapeDtypeStruct((B, S, G_WIDTH), BF16),
        compiler_params=_params(("parallel", "parallel"), 16),
        name="gmlp",
    )(z3, z3, gmlp_g, ws, bs_t)


def _merge_kernel(x_ref, gate_ref, ya_ref, yb_ref, yc_ref, wb_ref, wo_ref, gpost_ref, o_ref, *, sub):
    for r in range(x_ref.shape[0] // sub):
        rows = slice(r * sub, (r + 1) * sub)
        merged = None
        for n, y_ref in enumerate((ya_ref, yb_ref, yc_ref)):
            br = jnp.dot(y_ref[rows, :], wb_ref[n], preferred_element_type=F32)
            gt = jax.nn.sigmoid(gate_ref[rows, n * D_MODEL:(n + 1) * D_MODEL].astype(F32))
            merged = gt * br if merged is None else merged + gt * br
        y = jnp.dot(merged.astype(BF16), wo_ref[...], preferred_element_type=F32)
        o_ref[rows, :] = x_ref[rows, :] + _rms(y, gpost_ref[...])


def merge(x2, z2, y_fox, y_ml, y_g, w_branch, w_out, gains, *, layer, gain, tm=1024, sub=512):
    T, D = x2.shape
    tm = min(tm, T)
    row = lambda w: pl.BlockSpec((tm, w), lambda i: (i, 0))
    return pl.pallas_call(
        functools.partial(_merge_kernel, sub=min(sub, tm)),
        grid=(T // tm,),
        in_specs=[
            row(D),
            row(N_BRANCH * D),
            row(FOX_WIDTH), row(ML_WIDTH), row(G_WIDTH),
            _per_layer((N_BRANCH, FOX_WIDTH, D), layer),
            _per_layer((D, D), layer),
            _per_layer((1, D), gain),
        ],
        out_specs=row(D),
        out_shape=jax.ShapeDtypeStruct((T, D), F32),
        compiler_params=_params(("parallel",), 48),
        name="merge",
    )(x2, z2, y_fox, y_ml, y_g, w_branch, w_out, gains)


def _norm_matmul_kernel(x_ref, g_ref, w_ref, o_ref):
    h = _rms(x_ref[...], g_ref[...]).astype(BF16)
    o_ref[...] = jnp.dot(h, w_ref[...], preferred_element_type=F32).astype(o_ref.dtype)


def norm_matmul(x2, gains, w, *, layer, gain, tm=512):
    T, D = x2.shape
    N = w.shape[-1]
    tm = min(tm, T)
    return pl.pallas_call(
        _norm_matmul_kernel,
        grid=(T // tm,),
        in_specs=[
            pl.BlockSpec((tm, D), lambda i: (i, 0)),
            _per_layer((1, D), gain),
            _per_layer((D, N), layer),
        ],
        out_specs=pl.BlockSpec((tm, N), lambda i: (i, 0)),
        out_shape=jax.ShapeDtypeStruct((T, N), BF16),
        compiler_params=_params(("parallel",), 40),
        name="mem_kv",
    )(x2, gains, w)


def _xattn_kernel(x_ref, kv_ref, wq_ref, wo_ref, gpre_ref, gpost_ref, o_ref, *, sub):
    dn = (((1,), (1,)), ((), ()))
    for r in range(x_ref.shape[0] // sub):
        rows = slice(r * sub, (r + 1) * sub)
        x = x_ref[rows, :]
        h = _rms(x, gpre_ref[...]).astype(BF16)
        q = jnp.dot(h, wq_ref[...], preferred_element_type=F32).astype(BF16)
        outs = []
        for hd in range(X_HEADS):
            cols = slice(hd * X_HEAD_DIM, (hd + 1) * X_HEAD_DIM)
            k = kv_ref[:, cols]
            v = kv_ref[:, D_MODEL + hd * X_HEAD_DIM:D_MODEL + (hd + 1) * X_HEAD_DIM]
            s = lax.dot_general(q[:, cols], k, dn, preferred_element_type=F32) * (X_HEAD_DIM ** -0.5)
            s = s - jnp.max(s, axis=-1, keepdims=True)
            e = jnp.exp(s)
            p = e / jnp.sum(e, axis=-1, keepdims=True)
            outs.append(jnp.dot(p.astype(BF16), v, preferred_element_type=F32).astype(BF16))
        o = jnp.concatenate(outs, axis=-1)
        y = jnp.dot(o, wo_ref[...], preferred_element_type=F32)
        o_ref[rows, :] = x + _rms(y, gpost_ref[...])


def xattn(x3, kv3, w_q, w_o, gains, *, layer, gain_pre, gain_post, tm=1024, sub=512):
    B, S, D = x3.shape
    M = kv3.shape[1]
    tm = min(tm, S)
    return pl.pallas_call(
        functools.partial(_xattn_kernel, sub=min(sub, tm)),
        grid=(B, S // tm),
        in_specs=[
            pl.BlockSpec((None, tm, D), lambda b, i: (b, i, 0)),
            pl.BlockSpec((None, M, 2 * D), lambda b, i: (b, 0, 0)),
            _per_layer((D, D), layer), _per_layer((D, D), layer),
            _per_layer((1, D), gain_pre), _per_layer((1, D), gain_post),
        ],
        out_specs=pl.BlockSpec((None, tm, D), lambda b, i: (b, i, 0)),
        out_shape=jax.ShapeDtypeStruct((B, S, D), F32),
        compiler_params=_params(("parallel", "arbitrary"), 48),
        name="xattn",
    )(x3, kv3, w_q, w_o, gains, gains)


def _ffn_kernel(x_ref, w1_ref, w2_ref, gpre_ref, gpost_ref, o_ref, *, ff_chunk, sub):
    for r in range(x_ref.shape[0] // sub):
        rows = slice(r * sub, (r + 1) * sub)
        x = x_ref[rows, :]
        h = _rms(x, gpre_ref[...]).astype(BF16)
        acc = None
        for c in range(w1_ref.shape[1] // ff_chunk):
            cs = slice(c * ff_chunk, (c + 1) * ff_chunk)
            a = jnp.dot(h, w1_ref[:, cs], preferred_element_type=F32)
            a = jnp.square(jnp.maximum(a, 0.0)).astype(BF16)
            part = jnp.dot(a, w2_ref[cs, :], preferred_element_type=F32)
            acc = part if acc is None else acc + part
        o_ref[rows, :] = x + _rms(acc, gpost_ref[...])


def ffn(x2, w1, w2, gains, *, layer, gain_pre, gain_post, tm=1024, sub=512, ff_chunk=1024):
    T, D = x2.shape
    FF = w1.shape[-1]
    tm = min(tm, T)
    return pl.pallas_call(
        functools.partial(_ffn_kernel, ff_chunk=ff_chunk, sub=min(sub, tm)),
        grid=(T // tm,),
        in_specs=[
            pl.BlockSpec((tm, D), lambda i: (i, 0)),
            _per_layer((D, FF), layer),
            _per_layer((FF, D), layer),
            _per_layer((1, D), gain_pre),
            _per_layer((1, D), gain_post),
        ],
        out_specs=pl.BlockSpec((tm, D), lambda i: (i, 0)),
        out_shape=jax.ShapeDtypeStruct((T, D), F32),
        compiler_params=_params(("parallel",), 56),
        name="ffn",
    )(x2, w1, w2, gains, gains)


def _regroup_kernel(w_ref, wide_ref, narrow_ref):
    for name in _WIDE_ORDER:
        src, size = _IN_OFFSETS[name]
        dst = _WIDE_OFFSETS[name]
        wide_ref[:, dst:dst + size] = w_ref[:, src:src + size].astype(BF16)
    narrow_ref[...] = jnp.zeros_like(narrow_ref)
    dst = 0
    for name in _NARROW_ORDER:
        src, size = _IN_OFFSETS[name]
        narrow_ref[:, dst:dst + size] = w_ref[:, src:src + size].astype(BF16)
        dst += size


def _regroup_in_proj(w_in, b_in, *, rows=256):
    depth, D, d_in = w_in.shape
    w_wide, w_narrow = pl.pallas_call(
        _regroup_kernel,
        grid=(depth, D // rows),
        in_specs=[pl.BlockSpec((None, rows, d_in), lambda l, i: (l, i, 0))],
        out_specs=[pl.BlockSpec((None, rows, N_WIDE), lambda l, i: (l, i, 0)),
                   pl.BlockSpec((None, rows, LANES), lambda l, i: (l, i, 0))],
        out_shape=[jax.ShapeDtypeStruct((depth, D, N_WIDE), BF16),
                   jax.ShapeDtypeStruct((depth, D, LANES), BF16)],
        compiler_params=_params(("parallel", "parallel"), 48),
        name="regroup",
    )(w_in)

    def cols(names):
        return [slice(_IN_OFFSETS[n][0], _IN_OFFSETS[n][0] + _IN_OFFSETS[n][1]) for n in names]
    b_wide = jnp.concatenate([b_in[..., s] for s in cols(_WIDE_ORDER)], axis=-1)[:, None, :]
    pad = LANES - N_GATE_LANES
    b_narrow = jnp.pad(jnp.concatenate([b_in[..., s] for s in cols(_NARROW_ORDER)], axis=-1),
                       ((0, 0), (0, pad)))[:, None, :]
    return w_wide, b_wide, w_narrow, b_narrow


def _prepare_params(norms, w_in, b_in, conv_w, mlstm_norm, gmlp_norm, gmlp_ws, gmlp_bs,
                    w_branch, w_out, w_xq, w_xkv, w_xo, w_ff1, w_ff2):
    depth = norms.shape[0]
    w_wide, b_wide, w_narrow, b_narrow = _regroup_in_proj(w_in, b_in)
    return dict(
        gains=norms.reshape(depth * N_NORMS, 1, D_MODEL),
        w_wide=w_wide, b_wide=b_wide, w_narrow=w_narrow, b_narrow=b_narrow,
        conv_w=conv_w, mlstm_g=mlstm_norm[:, :, None], gmlp_g=gmlp_norm[:, None, :],
        gmlp_ws=gmlp_ws, gmlp_bs_t=gmlp_bs.transpose(0, 2, 1),
        w_branch=w_branch.astype(BF16), w_out=w_out.astype(BF16),
        w_xq=w_xq.astype(BF16), w_xkv=w_xkv.astype(BF16), w_xo=w_xo.astype(BF16),
        w_ff1=w_ff1.astype(BF16), w_ff2=w_ff2.astype(BF16),
    )


def _layer(x3, mem2, p, layer, *, fox_blk, ml_chunk):
    B, S, D = x3.shape
    T = B * S
    M = mem2.shape[0] // B
    gain = lambda idx: layer * N_NORMS + idx
    x2 = x3.reshape(T, D)

    z2, zn2 = in_proj(x2, p["gains"], p["w_wide"], p["b_wide"], p["w_narrow"], p["b_narrow"],
                      layer=layer, gain=gain(NORM_MIX_PRE))
    z3 = z2.reshape(B, S, N_WIDE)
    g3, gt4 = gates(zn2.reshape(B, S, LANES), L=ml_chunk)

    y_fox = fox_attention(z3, g3, blk=fox_blk)
    y_ml = mlstm(z3, p["conv_w"], p["mlstm_g"], g3, gt4, layer=layer, L=ml_chunk)
    y_g = gmlp(z3, p["gmlp_g"], p["gmlp_ws"], p["gmlp_bs_t"], layer=layer)
    x2 = merge(x2, z2, y_fox.reshape(T, FOX_WIDTH), y_ml.reshape(T, ML_WIDTH), y_g.reshape(T, G_WIDTH),
               p["w_branch"], p["w_out"], p["gains"], layer=layer, gain=gain(NORM_MIX_POST))

    kv = norm_matmul(mem2, p["gains"], p["w_xkv"], layer=layer, gain=gain(NORM_MEM))
    x3 = xattn(x2.reshape(B, S, D), kv.reshape(B, M, 2 * D), p["w_xq"], p["w_xo"], p["gains"],
               layer=layer, gain_pre=gain(NORM_X_PRE), gain_post=gain(NORM_X_POST))

    x2 = ffn(x3.reshape(T, D), p["w_ff1"], p["w_ff2"], p["gains"],
             layer=layer, gain_pre=gain(NORM_FF_PRE), gain_post=gain(NORM_FF_POST))
    return x2.reshape(B, S, D)


def kernel(x, mem, norms, w_in, b_in, conv_w, mlstm_norm, gmlp_norm, gmlp_ws, gmlp_bs, w_branch, w_out, w_xq, w_xkv, w_xo, w_ff1, w_ff2):
    B, M, D = mem.shape
    S = x.shape[1]
    mem2 = mem.reshape(B * M, D)
    p = _prepare_params(norms, w_in, b_in, conv_w, mlstm_norm, gmlp_norm, gmlp_ws, gmlp_bs,
                        w_branch, w_out, w_xq, w_xkv, w_xo, w_ff1, w_ff2)
    for layer in range(norms.shape[0]):
        x = _layer(x, mem2, p, layer, fox_blk=min(256, S), ml_chunk=min(ML_CHUNK, S))
    return x
```

```python
import functools

import jax
import jax.numpy as jnp
import numpy as np
from jax import lax
from jax.experimental import pallas as pl
from jax.experimental.pallas import tpu as pltpu

F32 = jnp.float32
BF16 = jnp.bfloat16

EPS = 1e-6
D_MODEL = 1024
CHUNK = 64
FOX_HEADS = 8
FOX_HEAD_DIM = 64
FOX_WIDTH = FOX_HEADS * FOX_HEAD_DIM
ML_HEADS = 4
ML_HEAD_DIM = 128
ML_WIDTH = ML_HEADS * ML_HEAD_DIM
ML_CHUNK = 128
ML_HALO = 8
CONV_WIDTH = 4
G_GROUPS = 4
G_GROUP_DIM = 128
G_WIDTH = G_GROUPS * G_GROUP_DIM
G_SPAN = 128
N_BRANCH = 3
X_HEADS = 4
X_HEAD_DIM = D_MODEL // X_HEADS
D_FF = 4 * D_MODEL
NORM_MIX_PRE, NORM_MIX_POST, NORM_X_PRE, NORM_X_POST, NORM_MEM, NORM_FF_PRE, NORM_FF_POST = range(7)
N_NORMS = 7

LOG2E = 1.4426950408889634
LANES = 128
MIB = 1024 * 1024

_IN_SPLITS = (
    ("fox_q", FOX_WIDTH), ("fox_k", FOX_WIDTH), ("fox_v", FOX_WIDTH), ("fox_f", FOX_HEADS),
    ("ml_q", ML_WIDTH), ("ml_k", ML_WIDTH), ("ml_v", ML_WIDTH),
    ("ml_i", ML_HEADS), ("ml_f", ML_HEADS), ("ml_o", ML_WIDTH),
    ("g_u", G_WIDTH), ("g_v", G_WIDTH),
    ("gate", N_BRANCH * D_MODEL),
)
_IN_OFFSETS = {}
_off = 0
for _name, _size in _IN_SPLITS:
    _IN_OFFSETS[_name] = (_off, _size)
    _off += _size

_WIDE_ORDER = ("gate", "fox_q", "fox_k", "fox_v", "ml_q", "ml_k", "ml_v", "ml_o", "g_u", "g_v")
_WIDE_OFFSETS = {}
_off = 0
for _name in _WIDE_ORDER:
    _WIDE_OFFSETS[_name] = _off
    _off += _IN_OFFSETS[_name][1]
N_WIDE = _off
_NARROW_ORDER = ("fox_f", "ml_i", "ml_f")
LANE_FOX_F = 0
LANE_ML_I = FOX_HEADS
LANE_ML_F = FOX_HEADS + ML_HEADS
N_GATE_LANES = FOX_HEADS + 2 * ML_HEADS


def _rms(x, g):
    return x * lax.rsqrt(jnp.mean(x * x, axis=-1, keepdims=True) + EPS) * g


def _log_sigmoid(x):
    return jnp.minimum(x, 0.0) - jnp.log1p(jnp.exp(-jnp.abs(x)))


def _params(semantics, vmem_mib):
    return pltpu.CompilerParams(dimension_semantics=semantics, vmem_limit_bytes=vmem_mib * MIB)


def _per_layer(block, layer, tail=None):
    block = tuple(block)
    if tail is None:
        return pl.BlockSpec((None,) + block, lambda *ids: (layer,) + (0,) * len(block),
                            pipeline_mode=pl.Buffered(1))
    return pl.BlockSpec((None,) + block, lambda *ids: (layer,) + tuple(tail(*ids)))


def _in_proj_kernel(x_ref, g_ref, wt_ref, b_ref, wnt_ref, bn_ref, z_ref, zn_ref, h_ref, *, sub):
    j = pl.program_id(1)
    nt = (((1,), (1,)), ((), ()))

    @pl.when(j == 0)
    def _():
        for r in range(x_ref.shape[0] // sub):
            rows = slice(r * sub, (r + 1) * sub)
            h = _rms(x_ref[rows, :], g_ref[...]).astype(BF16)
            h_ref[rows, :] = h
            zn_ref[rows, :] = lax.dot_general(h, wnt_ref[...], nt, preferred_element_type=F32) + bn_ref[...]
            z = lax.dot_general(h, wt_ref[...], nt, preferred_element_type=F32) + b_ref[...]
            z_ref[rows, :] = z.astype(z_ref.dtype)

    @pl.when(j != 0)
    def _():
        z = lax.dot_general(h_ref[...], wt_ref[...], nt, preferred_element_type=F32) + b_ref[...]
        z_ref[...] = z.astype(z_ref.dtype)


def in_proj(x2, gains, w_wide, b_wide, w_narrow, b_narrow, *, layer, gain, tm=2048, tn=1536, sub=512):
    T, D = x2.shape
    N = w_wide.shape[1]
    tm = min(tm, T)
    return pl.pallas_call(
        functools.partial(_in_proj_kernel, sub=min(sub, tm)),
        grid=(T // tm, N // tn),
        in_specs=[
            pl.BlockSpec((tm, D), lambda i, j: (i, 0)),
            _per_layer((1, D), gain),
            _per_layer((tn, D), layer, lambda i, j: (j, 0)),
            _per_layer((1, tn), layer, lambda i, j: (0, j)),
            _per_layer((LANES, D), layer),
            _per_layer((1, LANES), layer),
        ],
        out_specs=[
            pl.BlockSpec((tm, tn), lambda i, j: (i, j)),
            pl.BlockSpec((tm, LANES), lambda i, j: (i, 0)),
        ],
        out_shape=[
            jax.ShapeDtypeStruct((T, N), BF16),
            jax.ShapeDtypeStruct((T, LANES), F32),
        ],
        scratch_shapes=[pltpu.VMEM((tm, D), BF16)],
        compiler_params=_params(("parallel", "arbitrary"), 56),
        name="in_proj",
    )(x2, gains, w_wide, b_wide, w_narrow, b_narrow)


def _gates_kernel(zn_ref, g_ref, gt_ref, *, rows):
    S = zn_ref.shape[0]
    L = gt_ref.shape[-1]
    lane = lax.broadcasted_iota(jnp.int32, (rows, LANES), 1)
    keep_raw = jnp.logical_and(lane >= LANE_ML_I, lane < LANE_ML_F)
    r = lax.broadcasted_iota(jnp.int32, (rows, rows), 0)
    c = lax.broadcasted_iota(jnp.int32, (rows, rows), 1)
    tril = jnp.where(r >= c, 1.0, 0.0).astype(F32)

    offset = jnp.zeros((1, LANES), F32)
    for i in range(S // rows):
        sl = slice(i * rows, (i + 1) * rows)
        z = zn_ref[sl, :]
        ls = _log_sigmoid(z)
        cs = jnp.dot(tril, ls, precision=lax.Precision.HIGHEST, preferred_element_type=F32) + offset
        out = jnp.where(keep_raw, z, cs)
        g_ref[sl, :] = out
        out_t = out.T
        for k in range(rows // L):
            gt_ref[i * (rows // L) + k] = out_t[:N_GATE_LANES, k * L:(k + 1) * L]
        offset = cs[rows - 1:rows, :]


def gates(zn3, *, L, rows=256):
    B, S, _ = zn3.shape
    rows = min(rows, S)
    assert rows % L == 0
    return pl.pallas_call(
        functools.partial(_gates_kernel, rows=rows),
        grid=(B,),
        in_specs=[pl.BlockSpec((None, S, LANES), lambda b: (b, 0, 0))],
        out_specs=[pl.BlockSpec((None, S, LANES), lambda b: (b, 0, 0)),
                   pl.BlockSpec((None, S // L, N_GATE_LANES, L), lambda b: (b, 0, 0, 0))],
        out_shape=[jax.ShapeDtypeStruct((B, S, LANES), F32),
                   jax.ShapeDtypeStruct((B, S // L, N_GATE_LANES, L), F32)],
        compiler_params=_params(("parallel",), 16),
        name="gates",
    )(zn3)


FOX_AUG = LANES
FOX_SUB = 8
FOX_PAIR = 2 * FOX_HEAD_DIM
FOX_PAIRS = FOX_HEADS // 2
FOX_VT_ROWS = FOX_HEAD_DIM + 16


def _fox_place_matrices():
    pk = np.zeros((3 * LANES, FOX_PAIRS * FOX_AUG), np.float32)
    pq = np.zeros((3 * LANES, FOX_PAIRS * FOX_AUG), np.float32)
    ck = np.zeros((1, FOX_PAIRS * FOX_AUG), np.float32)
    cq = np.zeros((1, FOX_PAIRS * FOX_AUG), np.float32)
    for h in range(FOX_HEADS):
        base = (h // 2) * FOX_AUG + (h % 2) * FOX_SUB
        for piece in range(3):
            pk[piece * LANES + LANE_FOX_F + h, base + piece] = -1.0
            pq[piece * LANES + LANE_FOX_F + h, base + 3 + piece] = 1.0
            ck[0, base + 3 + piece] = 1.0
            cq[0, base + piece] = 1.0
    return (jnp.asarray(pk, BF16), jnp.asarray(pq, BF16), jnp.asarray(ck), jnp.asarray(cq))


def _fox_bias_lanes(f, place_ref, ones_ref):
    f = f * LOG2E
    hi = f.astype(BF16)
    r1 = f - hi.astype(F32)
    mid = r1.astype(BF16)
    lo = (r1 - mid.astype(F32)).astype(BF16)
    x = jnp.concatenate([hi, mid, lo], axis=-1)
    return (jnp.dot(x, place_ref[...], preferred_element_type=F32) + ones_ref[...]).astype(BF16)


def _fox_kernel(q_ref, k_ref, v_ref, g_ref, pk_ref, pq_ref, ck_ref, cq_ref, o_ref,
                kaug_ref, vt_ref, qaug_ref, m_ref, acc_ref, al_ref, st_ref, p_ref, *, blk, qblk):
    S = k_ref.shape[0]
    ratio = qblk // blk
    qi = pl.program_id(1)
    key_pos = lax.broadcasted_iota(jnp.int32, (blk, qblk), 0)
    qry_pos = lax.broadcasted_iota(jnp.int32, (blk, qblk), 1)

    @pl.when(qi == 0)
    def _():
        ones = jnp.ones((FOX_VT_ROWS - FOX_HEAD_DIM, blk), BF16)
        for c in range(S // blk):
            rows = slice(c * blk, (c + 1) * blk)
            kaug_ref[rows, :] = _fox_bias_lanes(g_ref[rows, :], pk_ref, ck_ref)
            for j in range(FOX_PAIRS):
                vt = v_ref[rows, j * FOX_PAIR:(j + 1) * FOX_PAIR].T
                for hh in range(2):
                    vt_ref[c, 2 * j + hh, :FOX_HEAD_DIM, :] = vt[hh * FOX_HEAD_DIM:(hh + 1) * FOX_HEAD_DIM, :]
                    vt_ref[c, 2 * j + hh, FOX_HEAD_DIM:, :] = ones

    q_rows = pl.ds(pl.multiple_of(qi * qblk, qblk), qblk)
    qaug = _fox_bias_lanes(g_ref[q_rows, :], pq_ref, cq_ref)
    q_half = lax.broadcasted_iota(jnp.int32, (qblk, FOX_PAIR), 1) // FOX_HEAD_DIM
    aug_half = lax.broadcasted_iota(jnp.int32, (qblk, FOX_AUG), 1) // FOX_SUB
    zero = jnp.zeros((), BF16)
    scale = FOX_HEAD_DIM ** -0.5 * LOG2E
    for j in range(FOX_PAIRS):
        qs = (q_ref[:, j * FOX_PAIR:(j + 1) * FOX_PAIR].astype(F32) * scale).astype(BF16)
        qa = qaug[:, j * FOX_AUG:(j + 1) * FOX_AUG]
        both = [jnp.concatenate([jnp.where(q_half == hh, qs, zero), jnp.where(aug_half == hh, qa, zero)], axis=-1).T
                for hh in range(2)]
        qaug_ref[j] = jnp.concatenate(both, axis=-1)
    m_ref[...] = jnp.full_like(m_ref, -jnp.inf)
    acc_ref[...] = jnp.zeros_like(acc_ref)

    def score_pair(kb, slot, j):
        ks = pl.ds(pl.multiple_of(kb * blk, blk), blk)
        kk = jnp.concatenate([k_ref[ks, j * FOX_PAIR:(j + 1) * FOX_PAIR],
                              kaug_ref[ks, j * FOX_AUG:(j + 1) * FOX_AUG]], axis=-1)
        st_ref[slot, j] = jnp.dot(kk, qaug_ref[j], preferred_element_type=F32)

    def softmax_head(slot, h, diag):
        st = st_ref[slot, h // 2, :, (h % 2) * qblk:(h % 2 + 1) * qblk]
        if diag is not None:
            st = jnp.where(key_pos + diag * blk <= qry_pos, st, -jnp.inf)
        m_old = m_ref[h:h + 1, :]
        m_new = jnp.maximum(m_old, jnp.max(st, axis=0, keepdims=True))
        al_ref[h:h + 1, :] = jnp.exp2(m_old - m_new)
        m_ref[h:h + 1, :] = m_new
        p_ref[h] = jnp.exp2(st - m_new).astype(BF16)

    def value_head(kb, h):
        pv = jnp.dot(vt_ref[kb, h], p_ref[h], preferred_element_type=F32)
        acc_ref[h] = al_ref[h:h + 1, :] * acc_ref[h] + pv

    def advance(kb, slot, diag, prefetch):
        for j in range(FOX_PAIRS):
            if prefetch:
                score_pair(kb + 1, 1 - slot, j)
            softmax_head(slot, 2 * j, diag)
            softmax_head(slot, 2 * j + 1, diag)
            if j > 0:
                value_head(kb, 2 * j - 2)
                value_head(kb, 2 * j - 1)
        value_head(kb, FOX_HEADS - 2)
        value_head(kb, FOX_HEADS - 1)

    for j in range(FOX_PAIRS):
        score_pair(0, 0, j)

    def two_blocks(i, carry):
        advance(2 * i, 0, None, True)
        advance(2 * i + 1, 1, None, True)
        return carry

    first_diag = ratio * qi
    lax.fori_loop(0, first_diag // 2, two_blocks, 0)
    for d in range(ratio):
        advance(first_diag + d, d % 2, d, d + 1 < ratio)

    outs = []
    for j in range(FOX_HEADS // 2):
        tops = []
        for h in (2 * j, 2 * j + 1):
            a = acc_ref[h]
            tops.append(a[:FOX_HEAD_DIM, :] / a[FOX_HEAD_DIM:FOX_HEAD_DIM + 1, :])
        outs.append(jnp.concatenate(tops, axis=0).T)
    o_ref[...] = jnp.concatenate(outs, axis=-1).astype(o_ref.dtype)


def fox_attention(z3, g3, *, blk=256, qblk=512):
    B, S, _ = z3.shape
    assert qblk % (2 * blk) == 0 and S % qblk == 0
    wq = _WIDE_OFFSETS["fox_q"] // FOX_WIDTH
    wk = _WIDE_OFFSETS["fox_k"] // FOX_WIDTH
    wv = _WIDE_OFFSETS["fox_v"] // FOX_WIDTH
    pk, pq, ck, cq = _fox_place_matrices()
    const = lambda a: pl.BlockSpec(a.shape, lambda b, i: (0,) * a.ndim)
    return pl.pallas_call(
        functools.partial(_fox_kernel, blk=blk, qblk=qblk),
        grid=(B, S // qblk),
        in_specs=[
            pl.BlockSpec((None, qblk, FOX_WIDTH), lambda b, i: (b, i, wq)),
            pl.BlockSpec((None, S, FOX_WIDTH), lambda b, i: (b, 0, wk)),
            pl.BlockSpec((None, S, FOX_WIDTH), lambda b, i: (b, 0, wv)),
            pl.BlockSpec((None, S, LANES), lambda b, i: (b, 0, 0)),
            const(pk), const(pq), const(ck), const(cq),
        ],
        out_specs=pl.BlockSpec((None, qblk, FOX_WIDTH), lambda b, i: (b, i, 0)),
        out_shape=jax.ShapeDtypeStruct((B, S, FOX_WIDTH), BF16),
        scratch_shapes=[
            pltpu.VMEM((S, FOX_PAIRS * FOX_AUG), BF16),
            pltpu.VMEM((S // blk, FOX_HEADS, FOX_VT_ROWS, blk), BF16),
            pltpu.VMEM((FOX_PAIRS, FOX_PAIR + FOX_AUG, 2 * qblk), BF16),
            pltpu.VMEM((FOX_HEADS, qblk), F32),
            pltpu.VMEM((FOX_HEADS, FOX_VT_ROWS, qblk), F32),
            pltpu.VMEM((FOX_HEADS, qblk), F32),
            pltpu.VMEM((2, FOX_PAIRS, blk, 2 * qblk), F32),
            pltpu.VMEM((FOX_HEADS, blk, qblk), BF16),
        ],
        compiler_params=_params(("parallel", "arbitrary"), 48),
        name="fox",
    )(z3, z3, z3, g3, pk, pq, ck, cq)


def _mlstm_kernel(q_ref, k_ref, v_ref, o_ref, cw_ref, gn_ref, g_ref, gt_ref, y_ref,
                  cn_ref, st_ref, halo_ref, gnb_ref, *, L):
    S = q_ref.shape[0]
    HALO = halo_ref.shape[0]
    src = lax.broadcasted_iota(jnp.int32, (L, L), 0)
    qry = lax.broadcasted_iota(jnp.int32, (L, L), 1)
    triu = src <= qry
    lane = lax.broadcasted_iota(jnp.int32, (L, LANES), 1)
    ones_col = jnp.where(lane == 0, 1.0, 0.0).astype(BF16)
    k_scale = ML_HEAD_DIM ** -0.5
    dn_t = (((1,), (1,)), ((), ()))
    dn_0 = (((0,), (0,)), ((), ()))

    cn_ref[...] = jnp.zeros_like(cn_ref)
    st_ref[...] = jnp.zeros_like(st_ref)
    halo_ref[...] = jnp.zeros_like(halo_ref)
    for h in range(ML_HEADS):
        gnb_ref[h] = jnp.broadcast_to(gn_ref[h * ML_HEAD_DIM:(h + 1) * ML_HEAD_DIM, :], (ML_HEAD_DIM, L))

    def conv_silu(x_chunk, halo, w):
        xx = jnp.concatenate([halo, x_chunk], axis=0)
        y = w[CONV_WIDTH - 1:CONV_WIDTH, :] * x_chunk
        for j in range(CONV_WIDTH - 1):
            sh = CONV_WIDTH - 1 - j
            y = y + w[j:j + 1, :] * xx[HALO - sh:HALO - sh + L, :]
        return y * jax.nn.sigmoid(y)

    def chunk(ci, _):
        rows = pl.ds(pl.multiple_of(ci * L, L), L)
        xq = q_ref[rows, :].astype(F32)
        xk = k_ref[rows, :].astype(F32)
        qa = conv_silu(xq, halo_ref[:, :ML_WIDTH], cw_ref[:, :ML_WIDTH])
        ka = conv_silu(xk, halo_ref[:, ML_WIDTH:], cw_ref[:, ML_WIDTH:]) * k_scale
        halo_ref[:, :ML_WIDTH] = xq[L - HALO:, :]
        halo_ref[:, ML_WIDTH:] = xk[L - HALO:, :]
        gcol = g_ref[rows, :]
        grow = gt_ref[ci]

        for h in range(ML_HEADS):
            cols = slice(h * ML_HEAD_DIM, (h + 1) * ML_HEAD_DIM)
            qb = qa[:, cols].astype(BF16)
            kf = ka[:, cols]
            vb = v_ref[rows, cols]
            c_col = gcol[:, LANE_ML_I + h:LANE_ML_I + h + 1] - gcol[:, LANE_ML_F + h:LANE_ML_F + h + 1]
            f_r = grow[LANE_ML_F + h:LANE_ML_F + h + 1, :]
            f_prev = st_ref[h, 0:1, 0:1]
            m_prev = st_ref[h, 0:1, 1:2]

            d_t = jnp.where(triu, c_col + f_r, -jnp.inf)
            inter = f_r - f_prev + m_prev
            m = jnp.maximum(inter, jnp.max(d_t, axis=0, keepdims=True))
            w_inter = jnp.exp(inter - m)
            p_t = lax.dot_general(kf.astype(BF16), qb, dn_t, preferred_element_type=F32) * jnp.exp(d_t - m)
            io_t = lax.dot_general(cn_ref[h].astype(BF16), qb, dn_t, preferred_element_type=F32)
            pv_t = lax.dot_general(vb, p_t.astype(BF16), dn_0, preferred_element_type=F32)
            num = w_inter * io_t[:ML_HEAD_DIM, :] + pv_t
            den = w_inter * io_t[ML_HEAD_DIM:ML_HEAD_DIM + 1, :] + jnp.sum(p_t, axis=0, keepdims=True)
            h_t = num / jnp.maximum(jnp.abs(den), jnp.exp(-m))

            m_new = m[:, L - 1:L]
            f_end = f_r[:, L - 1:L]
            decay = jnp.exp(f_end - f_prev + m_prev - m_new)
            w_s = jnp.exp(jnp.broadcast_to(c_col, (L, ML_HEAD_DIM)) + (f_end - m_new))
            kw = (w_s * kf).astype(BF16)
            v_aug = jnp.concatenate([vb, ones_col], axis=-1)
            upd = lax.dot_general(v_aug, kw, dn_0, preferred_element_type=F32)
            cn_ref[h] = decay * cn_ref[h] + upd
            st_ref[h, 0:1, 0:1] = f_end
            st_ref[h, 0:1, 1:2] = m_new

            hn_t = h_t * lax.rsqrt(jnp.mean(h_t * h_t, axis=0, keepdims=True) + EPS) * gnb_ref[h]
            y = jax.nn.sigmoid(o_ref[rows, cols].astype(F32)) * hn_t.T
            y_ref[rows, cols] = y.astype(y_ref.dtype)
        return 0

    lax.fori_loop(0, S // L, chunk, 0)


def mlstm(z3, conv_w, mlstm_g, g3, gt4, *, layer, L=CHUNK):
    B, S, _ = z3.shape
    blocks = [_WIDE_OFFSETS[n] // ML_WIDTH for n in ("ml_q", "ml_k", "ml_v", "ml_o")]
    seq_spec = lambda idx: pl.BlockSpec((None, S, ML_WIDTH), lambda b: (b, 0, idx))
    return pl.pallas_call(
        functools.partial(_mlstm_kernel, L=L),
        grid=(B,),
        in_specs=[
            seq_spec(blocks[0]), seq_spec(blocks[1]), seq_spec(blocks[2]), seq_spec(blocks[3]),
            _per_layer((CONV_WIDTH, 2 * ML_WIDTH), layer),
            _per_layer((ML_WIDTH, 1), layer),
            pl.BlockSpec((None, S, LANES), lambda b: (b, 0, 0)),
            pl.BlockSpec((None, S // L, N_GATE_LANES, L), lambda b: (b, 0, 0, 0)),
        ],
        out_specs=pl.BlockSpec((None, S, ML_WIDTH), lambda b: (b, 0, 0)),
        out_shape=jax.ShapeDtypeStruct((B, S, ML_WIDTH), BF16),
        scratch_shapes=[
            pltpu.VMEM((ML_HEADS, 2 * ML_HEAD_DIM, ML_HEAD_DIM), F32),
            pltpu.VMEM((ML_HEADS, 8, LANES), F32),
            pltpu.VMEM((ML_HALO, 2 * ML_WIDTH), F32),
            pltpu.VMEM((ML_HEADS, ML_HEAD_DIM, L), F32),
        ],
        compiler_params=_params(("parallel",), 40),
        name="mlstm",
    )(z3, z3, z3, z3, conv_w, mlstm_g, g3, gt4)


def _gelu_tanh(x):
    c0 = -2.0 * np.sqrt(2.0 / np.pi) * LOG2E
    c1 = c0 * 0.044715
    return x / (1.0 + jnp.exp2(x * (c0 + c1 * (x * x))))


def _gmlp_kernel(u_ref, v_ref, gn_ref, ws_ref, bst_ref, y_ref):
    rows = u_ref.shape[0]
    u = _gelu_tanh(u_ref[...].astype(F32))
    v = _gelu_tanh(v_ref[...].astype(F32))
    mu = jnp.mean(v, axis=-1, keepdims=True)
    vc = v - mu
    vn = (vc * lax.rsqrt(jnp.mean(vc * vc, axis=-1, keepdims=True) + EPS) * gn_ref[...]).astype(BF16)
    r = lax.broadcasted_iota(jnp.int32, (G_SPAN, G_SPAN), 0) // CHUNK
    c = lax.broadcasted_iota(jnp.int32, (G_SPAN, G_SPAN), 1) // CHUNK
    mask = r >= c
    for g in range(G_GROUPS):
        cols = slice(g * G_GROUP_DIM, (g + 1) * G_GROUP_DIM)
        w = jnp.where(mask, ws_ref[g], 0.0).astype(BF16)
        bias = bst_ref[:, g:g + 1]
        for s in range(rows // G_SPAN):
            rs = slice(s * G_SPAN, (s + 1) * G_SPAN)
            mixed = jnp.dot(w, vn[rs, cols], preferred_element_type=F32) + bias
            y_ref[rs, cols] = (u[rs, cols] * mixed).astype(y_ref.dtype)


def gmlp(z3, gmlp_g, ws, bs_t, *, layer, rows=512):
    B, S, _ = z3.shape
    rows = min(rows, S)
    bu = _WIDE_OFFSETS["g_u"] // G_WIDTH
    bv = _WIDE_OFFSETS["g_v"] // G_WIDTH
    return pl.pallas_call(
        _gmlp_kernel,
        grid=(B, S // rows),
        in_specs=[
            pl.BlockSpec((None, rows, G_WIDTH), lambda b, i: (b, i, bu)),
            pl.BlockSpec((None, rows, G_WIDTH), lambda b, i: (b, i, bv)),
            _per_layer((1, G_WIDTH), layer),
            _per_layer((G_GROUPS, G_SPAN, G_SPAN), layer),
            _per_layer((G_SPAN, G_GROUPS), layer),
        ],
        out_specs=pl.BlockSpec((None, rows, G_WIDTH), lambda b, i: (b, i, 0)),
        out_shape=jax.ShapeDtypeStruct((B, S, G_WIDTH), BF16),
        compiler_params=_params(("parallel", "parallel"), 16),
        name="gmlp",
    )(z3, z3, gmlp_g, ws, bs_t)


def _merge_kernel(x_ref, gate_ref, ya_ref, yb_ref, yc_ref, wb_ref, wo_ref, gpost_ref, o_ref, *, sub):
    for r in range(x_ref.shape[0] // sub):
        rows = slice(r * sub, (r + 1) * sub)
        merged = None
        for n, y_ref in enumerate((ya_ref, yb_ref, yc_ref)):
            br = jnp.dot(y_ref[rows, :], wb_ref[n], preferred_element_type=F32)
            gt = jax.nn.sigmoid(gate_ref[rows, n * D_MODEL:(n + 1) * D_MODEL].astype(F32))
            merged = gt * br if merged is None else merged + gt * br
        y = jnp.dot(merged.astype(BF16), wo_ref[...], preferred_element_type=F32)
        o_ref[rows, :] = x_ref[rows, :] + _rms(y, gpost_ref[...])


def merge(x2, z2, y_fox, y_ml, y_g, w_branch, w_out, gains, *, layer, gain, tm=1024, sub=512):
    T, D = x2.shape
    tm = min(tm, T)
    row = lambda w: pl.BlockSpec((tm, w), lambda i: (i, 0))
    return pl.pallas_call(
        functools.partial(_merge_kernel, sub=min(sub, tm)),
        grid=(T // tm,),
        in_specs=[
            row(D),
            row(N_BRANCH * D),
            row(FOX_WIDTH), row(ML_WIDTH), row(G_WIDTH),
            _per_layer((N_BRANCH, FOX_WIDTH, D), layer),
            _per_layer((D, D), layer),
            _per_layer((1, D), gain),
        ],
        out_specs=row(D),
        out_shape=jax.ShapeDtypeStruct((T, D), F32),
        compiler_params=_params(("parallel",), 48),
        name="merge",
    )(x2, z2, y_fox, y_ml, y_g, w_branch, w_out, gains)


def _norm_matmul_kernel(x_ref, g_ref, w_ref, o_ref):
    h = _rms(x_ref[...], g_ref[...]).astype(BF16)
    o_ref[...] = jnp.dot(h, w_ref[...], preferred_element_type=F32).astype(o_ref.dtype)


def norm_matmul(x2, gains, w, *, layer, gain, tm=512):
    T, D = x2.shape
    N = w.shape[-1]
    tm = min(tm, T)
    return pl.pallas_call(
        _norm_matmul_kernel,
        grid=(T // tm,),
        in_specs=[
            pl.BlockSpec((tm, D), lambda i: (i, 0)),
            _per_layer((1, D), gain),
            _per_layer((D, N), layer),
        ],
        out_specs=pl.BlockSpec((tm, N), lambda i: (i, 0)),
        out_shape=jax.ShapeDtypeStruct((T, N), BF16),
        compiler_params=_params(("parallel",), 40),
        name="mem_kv",
    )(x2, gains, w)


def _xattn_kernel(x_ref, kv_ref, wq_ref, wo_ref, gpre_ref, gpost_ref, o_ref, *, sub):
    dn = (((1,), (1,)), ((), ()))
    for r in range(x_ref.shape[0] // sub):
        rows = slice(r * sub, (r + 1) * sub)
        x = x_ref[rows, :]
        h = _rms(x, gpre_ref[...]).astype(BF16)
        q = jnp.dot(h, wq_ref[...], preferred_element_type=F32).astype(BF16)
        outs = []
        for hd in range(X_HEADS):
            cols = slice(hd * X_HEAD_DIM, (hd + 1) * X_HEAD_DIM)
            k = kv_ref[:, cols]
            v = kv_ref[:, D_MODEL + hd * X_HEAD_DIM:D_MODEL + (hd + 1) * X_HEAD_DIM]
            s = lax.dot_general(q[:, cols], k, dn, preferred_element_type=F32) * (X_HEAD_DIM ** -0.5)
            s = s - jnp.max(s, axis=-1, keepdims=True)
            e = jnp.exp(s)
            p = e / jnp.sum(e, axis=-1, keepdims=True)
            outs.append(jnp.dot(p.astype(BF16), v, preferred_element_type=F32).astype(BF16))
        o = jnp.concatenate(outs, axis=-1)
        y = jnp.dot(o, wo_ref[...], preferred_element_type=F32)
        o_ref[rows, :] = x + _rms(y, gpost_ref[...])


def xattn(x3, kv3, w_q, w_o, gains, *, layer, gain_pre, gain_post, tm=1024, sub=512):
    B, S, D = x3.shape
    M = kv3.shape[1]
    tm = min(tm, S)
    return pl.pallas_call(
        functools.partial(_xattn_kernel, sub=min(sub, tm)),
        grid=(B, S // tm),
        in_specs=[
            pl.BlockSpec((None, tm, D), lambda b, i: (b, i, 0)),
            pl.BlockSpec((None, M, 2 * D), lambda b, i: (b, 0, 0)),
            _per_layer((D, D), layer), _per_layer((D, D), layer),
            _per_layer((1, D), gain_pre), _per_layer((1, D), gain_post),
        ],
        out_specs=pl.BlockSpec((None, tm, D), lambda b, i: (b, i, 0)),
        out_shape=jax.ShapeDtypeStruct((B, S, D), F32),
        compiler_params=_params(("parallel", "arbitrary"), 48),
        name="xattn",
    )(x3, kv3, w_q, w_o, gains, gains)


def _ffn_kernel(x_ref, w1_ref, w2_ref, gpre_ref, gpost_ref, o_ref, *, ff_chunk, sub):
    for r in range(x_ref.shape[0] // sub):
        rows = slice(r * sub, (r + 1) * sub)
        x = x_ref[rows, :]
        h = _rms(x, gpre_ref[...]).astype(BF16)
        acc = None
        for c in range(w1_ref.shape[1] // ff_chunk):
            cs = slice(c * ff_chunk, (c + 1) * ff_chunk)
            a = jnp.dot(h, w1_ref[:, cs], preferred_element_type=F32)
            a = jnp.square(jnp.maximum(a, 0.0)).astype(BF16)
            part = jnp.dot(a, w2_ref[cs, :], preferred_element_type=F32)
            acc = part if acc is None else acc + part
        o_ref[rows, :] = x + _rms(acc, gpost_ref[...])


def ffn(x2, w1, w2, gains, *, layer, gain_pre, gain_post, tm=1024, sub=512, ff_chunk=1024):
    T, D = x2.shape
    FF = w1.shape[-1]
    tm = min(tm, T)
    return pl.pallas_call(
        functools.partial(_ffn_kernel, ff_chunk=ff_chunk, sub=min(sub, tm)),
        grid=(T // tm,),
        in_specs=[
            pl.BlockSpec((tm, D), lambda i: (i, 0)),
            _per_layer((D, FF), layer),
            _per_layer((FF, D), layer),
            _per_layer((1, D), gain_pre),
            _per_layer((1, D), gain_post),
        ],
        out_specs=pl.BlockSpec((tm, D), lambda i: (i, 0)),
        out_shape=jax.ShapeDtypeStruct((T, D), F32),
        compiler_params=_params(("parallel",), 56),
        name="ffn",
    )(x2, w1, w2, gains, gains)


def _regroup_kernel(wt_ref, wide_ref, narrow_ref):
    for name in _WIDE_ORDER:
        src, size = _IN_OFFSETS[name]
        dst = _WIDE_OFFSETS[name]
        wide_ref[dst:dst + size, :] = wt_ref[src:src + size, :].astype(BF16)
    narrow_ref[...] = jnp.zeros_like(narrow_ref)
    dst = 0
    for name in _NARROW_ORDER:
        src, size = _IN_OFFSETS[name]
        narrow_ref[dst:dst + size, :] = wt_ref[src:src + size, :].astype(BF16)
        dst += size


def _regroup_in_proj(w_in, b_in, *, cols_per_step=256):
    depth, D, d_in = w_in.shape
    w_in_t = jnp.swapaxes(w_in, 1, 2)
    w_wide, w_narrow = pl.pallas_call(
        _regroup_kernel,
        grid=(depth, D // cols_per_step),
        in_specs=[pl.BlockSpec((None, d_in, cols_per_step), lambda l, i: (l, 0, i))],
        out_specs=[pl.BlockSpec((None, N_WIDE, cols_per_step), lambda l, i: (l, 0, i)),
                   pl.BlockSpec((None, LANES, cols_per_step), lambda l, i: (l, 0, i))],
        out_shape=[jax.ShapeDtypeStruct((depth, N_WIDE, D), BF16),
                   jax.ShapeDtypeStruct((depth, LANES, D), BF16)],
        compiler_params=_params(("parallel", "parallel"), 48),
        name="regroup",
    )(w_in_t)

    def cols(names):
        return [slice(_IN_OFFSETS[n][0], _IN_OFFSETS[n][0] + _IN_OFFSETS[n][1]) for n in names]
    b_wide = jnp.concatenate([b_in[..., s] for s in cols(_WIDE_ORDER)], axis=-1)[:, None, :]
    pad = LANES - N_GATE_LANES
    b_narrow = jnp.pad(jnp.concatenate([b_in[..., s] for s in cols(_NARROW_ORDER)], axis=-1),
                       ((0, 0), (0, pad)))[:, None, :]
    return w_wide, b_wide, w_narrow, b_narrow


def _prepare_params(norms, w_in, b_in, conv_w, mlstm_norm, gmlp_norm, gmlp_ws, gmlp_bs,
                    w_branch, w_out, w_xq, w_xkv, w_xo, w_ff1, w_ff2):
    depth = norms.shape[0]
    w_wide, b_wide, w_narrow, b_narrow = _regroup_in_proj(w_in, b_in)
    return dict(
        gains=norms.reshape(depth * N_NORMS, 1, D_MODEL),
        w_wide=w_wide, b_wide=b_wide, w_narrow=w_narrow, b_narrow=b_narrow,
        conv_w=conv_w, mlstm_g=mlstm_norm[:, :, None], gmlp_g=gmlp_norm[:, None, :],
        gmlp_ws=gmlp_ws, gmlp_bs_t=gmlp_bs.transpose(0, 2, 1),
        w_branch=w_branch.astype(BF16), w_out=w_out.astype(BF16),
        w_xq=w_xq.astype(BF16), w_xkv=w_xkv.astype(BF16), w_xo=w_xo.astype(BF16),
        w_ff1=w_ff1.astype(BF16), w_ff2=w_ff2.astype(BF16),
    )


def _layer(x3, mem2, p, layer, *, fox_blk, ml_chunk):
    B, S, D = x3.shape
    T = B * S
    M = mem2.shape[0] // B
    gain = lambda idx: layer * N_NORMS + idx
    x2 = x3.reshape(T, D)

    z2, zn2 = in_proj(x2, p["gains"], p["w_wide"], p["b_wide"], p["w_narrow"], p["b_narrow"],
                      layer=layer, gain=gain(NORM_MIX_PRE))
    z3 = z2.reshape(B, S, N_WIDE)
    g3, gt4 = gates(zn2.reshape(B, S, LANES), L=ml_chunk)

    y_fox = fox_attention(z3, g3, blk=fox_blk, qblk=min(2 * fox_blk, S))
    y_ml = mlstm(z3, p["conv_w"], p["mlstm_g"], g3, gt4, layer=layer, L=ml_chunk)
    y_g = gmlp(z3, p["gmlp_g"], p["gmlp_ws"], p["gmlp_bs_t"], layer=layer)
    x2 = merge(x2, z2, y_fox.reshape(T, FOX_WIDTH), y_ml.reshape(T, ML_WIDTH), y_g.reshape(T, G_WIDTH),
               p["w_branch"], p["w_out"], p["gains"], layer=layer, gain=gain(NORM_MIX_POST))

    kv = norm_matmul(mem2, p["gains"], p["w_xkv"], layer=layer, gain=gain(NORM_MEM))
    x3 = xattn(x2.reshape(B, S, D), kv.reshape(B, M, 2 * D), p["w_xq"], p["w_xo"], p["gains"],
               layer=layer, gain_pre=gain(NORM_X_PRE), gain_post=gain(NORM_X_POST))

    x2 = ffn(x3.reshape(T, D), p["w_ff1"], p["w_ff2"], p["gains"],
             layer=layer, gain_pre=gain(NORM_FF_PRE), gain_post=gain(NORM_FF_POST))
    return x2.reshape(B, S, D)


def kernel(x, mem, norms, w_in, b_in, conv_w, mlstm_norm, gmlp_norm, gmlp_ws, gmlp_bs, w_branch, w_out, w_xq, w_xkv, w_xo, w_ff1, w_ff2):
    B, M, D = mem.shape
    S = x.shape[1]
    mem2 = mem.reshape(B * M, D)
    p = _prepare_params(norms, w_in, b_in, conv_w, mlstm_norm, gmlp_norm, gmlp_ws, gmlp_bs,
                        w_branch, w_out, w_xq, w_xkv, w_xo, w_ff1, w_ff2)
    for layer in range(norms.shape[0]):
        x = _layer(x, mem2, p, layer, fox_blk=min(256, S), ml_chunk=min(ML_CHUNK, S))
    return x
```

```python
import functools

import jax
import jax.numpy as jnp
import numpy as np
from jax import lax
from jax.experimental import pallas as pl
from jax.experimental.pallas import tpu as pltpu

F32 = jnp.float32
BF16 = jnp.bfloat16

EPS = 1e-6
D_MODEL = 1024
CHUNK = 64
FOX_HEADS = 8
FOX_HEAD_DIM = 64
FOX_WIDTH = FOX_HEADS * FOX_HEAD_DIM
ML_HEADS = 4
ML_HEAD_DIM = 128
ML_WIDTH = ML_HEADS * ML_HEAD_DIM
ML_CHUNK = 128
ML_HALO = 8
CONV_WIDTH = 4
G_GROUPS = 4
G_GROUP_DIM = 128
G_WIDTH = G_GROUPS * G_GROUP_DIM
G_SPAN = 128
N_BRANCH = 3
X_HEADS = 4
X_HEAD_DIM = D_MODEL // X_HEADS
D_FF = 4 * D_MODEL
NORM_MIX_PRE, NORM_MIX_POST, NORM_X_PRE, NORM_X_POST, NORM_MEM, NORM_FF_PRE, NORM_FF_POST = range(7)
N_NORMS = 7

LOG2E = 1.4426950408889634
LANES = 128
MIB = 1024 * 1024

_IN_SPLITS = (
    ("fox_q", FOX_WIDTH), ("fox_k", FOX_WIDTH), ("fox_v", FOX_WIDTH), ("fox_f", FOX_HEADS),
    ("ml_q", ML_WIDTH), ("ml_k", ML_WIDTH), ("ml_v", ML_WIDTH),
    ("ml_i", ML_HEADS), ("ml_f", ML_HEADS), ("ml_o", ML_WIDTH),
    ("g_u", G_WIDTH), ("g_v", G_WIDTH),
    ("gate", N_BRANCH * D_MODEL),
)
_IN_OFFSETS = {}
_off = 0
for _name, _size in _IN_SPLITS:
    _IN_OFFSETS[_name] = (_off, _size)
    _off += _size

_WIDE_ORDER = ("gate", "fox_q", "fox_k", "fox_v", "ml_q", "ml_k", "ml_v", "ml_o", "g_u", "g_v")
_WIDE_OFFSETS = {}
_off = 0
for _name in _WIDE_ORDER:
    _WIDE_OFFSETS[_name] = _off
    _off += _IN_OFFSETS[_name][1]
N_WIDE = _off
_NARROW_ORDER = ("fox_f", "ml_i", "ml_f")
LANE_FOX_F = 0
LANE_ML_I = FOX_HEADS
LANE_ML_F = FOX_HEADS + ML_HEADS
N_GATE_LANES = FOX_HEADS + 2 * ML_HEADS


def _rms(x, g):
    return x * lax.rsqrt(jnp.mean(x * x, axis=-1, keepdims=True) + EPS) * g


def _log_sigmoid(x):
    return jnp.minimum(x, 0.0) - jnp.log1p(jnp.exp(-jnp.abs(x)))


def _params(semantics, vmem_mib):
    return pltpu.CompilerParams(dimension_semantics=semantics, vmem_limit_bytes=vmem_mib * MIB)


def _per_layer(block, layer, tail=None):
    block = tuple(block)
    if tail is None:
        return pl.BlockSpec((None,) + block, lambda *ids: (layer,) + (0,) * len(block),
                            pipeline_mode=pl.Buffered(1))
    return pl.BlockSpec((None,) + block, lambda *ids: (layer,) + tuple(tail(*ids)))


def _in_proj_kernel(x_ref, g_ref, wt_ref, b_ref, wnt_ref, bn_ref, z_ref, zn_ref, h_ref, *, sub):
    j = pl.program_id(1)
    nt = (((1,), (1,)), ((), ()))

    @pl.when(j == 0)
    def _():
        for r in range(x_ref.shape[0] // sub):
            rows = slice(r * sub, (r + 1) * sub)
            h = _rms(x_ref[rows, :], g_ref[...]).astype(BF16)
            h_ref[rows, :] = h
            zn_ref[rows, :] = lax.dot_general(h, wnt_ref[...], nt, preferred_element_type=F32) + bn_ref[...]
            z = lax.dot_general(h, wt_ref[...], nt, preferred_element_type=F32) + b_ref[...]
            z_ref[rows, :] = z.astype(z_ref.dtype)

    @pl.when(j != 0)
    def _():
        z = lax.dot_general(h_ref[...], wt_ref[...], nt, preferred_element_type=F32) + b_ref[...]
        z_ref[...] = z.astype(z_ref.dtype)


def in_proj(x2, gains, w_wide, b_wide, w_narrow, b_narrow, *, layer, gain, tm=2048, tn=1536, sub=512):
    T, D = x2.shape
    N = w_wide.shape[1]
    tm = min(tm, T)
    return pl.pallas_call(
        functools.partial(_in_proj_kernel, sub=min(sub, tm)),
        grid=(T // tm, N // tn),
        in_specs=[
            pl.BlockSpec((tm, D), lambda i, j: (i, 0)),
            _per_layer((1, D), gain),
            _per_layer((tn, D), layer, lambda i, j: (j, 0)),
            _per_layer((1, tn), layer, lambda i, j: (0, j)),
            _per_layer((LANES, D), layer),
            _per_layer((1, LANES), layer),
        ],
        out_specs=[
            pl.BlockSpec((tm, tn), lambda i, j: (i, j)),
            pl.BlockSpec((tm, LANES), lambda i, j: (i, 0)),
        ],
        out_shape=[
            jax.ShapeDtypeStruct((T, N), BF16),
            jax.ShapeDtypeStruct((T, LANES), F32),
        ],
        scratch_shapes=[pltpu.VMEM((tm, D), BF16)],
        compiler_params=_params(("parallel", "arbitrary"), 56),
        name="in_proj",
    )(x2, gains, w_wide, b_wide, w_narrow, b_narrow)


def _gates_kernel(zn_ref, g_ref, gt_ref, *, rows):
    S = zn_ref.shape[0]
    L = gt_ref.shape[-1]
    lane = lax.broadcasted_iota(jnp.int32, (rows, LANES), 1)
    keep_raw = jnp.logical_and(lane >= LANE_ML_I, lane < LANE_ML_F)
    r = lax.broadcasted_iota(jnp.int32, (rows, rows), 0)
    c = lax.broadcasted_iota(jnp.int32, (rows, rows), 1)
    tril = jnp.where(r >= c, 1.0, 0.0).astype(F32)

    offset = jnp.zeros((1, LANES), F32)
    for i in range(S // rows):
        sl = slice(i * rows, (i + 1) * rows)
        z = zn_ref[sl, :]
        ls = _log_sigmoid(z)
        cs = jnp.dot(tril, ls, precision=lax.Precision.HIGHEST, preferred_element_type=F32) + offset
        out = jnp.where(keep_raw, z, cs)
        g_ref[sl, :] = out
        out_t = out.T
        for k in range(rows // L):
            gt_ref[i * (rows // L) + k] = out_t[:N_GATE_LANES, k * L:(k + 1) * L]
        offset = cs[rows - 1:rows, :]


def gates(zn3, *, L, rows=256):
    B, S, _ = zn3.shape
    rows = min(rows, S)
    assert rows % L == 0
    return pl.pallas_call(
        functools.partial(_gates_kernel, rows=rows),
        grid=(B,),
        in_specs=[pl.BlockSpec((None, S, LANES), lambda b: (b, 0, 0))],
        out_specs=[pl.BlockSpec((None, S, LANES), lambda b: (b, 0, 0)),
                   pl.BlockSpec((None, S // L, N_GATE_LANES, L), lambda b: (b, 0, 0, 0))],
        out_shape=[jax.ShapeDtypeStruct((B, S, LANES), F32),
                   jax.ShapeDtypeStruct((B, S // L, N_GATE_LANES, L), F32)],
        compiler_params=_params(("parallel",), 16),
        name="gates",
    )(zn3)


FOX_AUG = LANES
FOX_SUB = 8
FOX_PAIR = 2 * FOX_HEAD_DIM
FOX_PAIRS = FOX_HEADS // 2
FOX_VT_ROWS = FOX_HEAD_DIM + 16


def _fox_place_matrices():
    pk = np.zeros((3 * LANES, FOX_PAIRS * FOX_AUG), np.float32)
    pq = np.zeros((3 * LANES, FOX_PAIRS * FOX_AUG), np.float32)
    ck = np.zeros((1, FOX_PAIRS * FOX_AUG), np.float32)
    cq = np.zeros((1, FOX_PAIRS * FOX_AUG), np.float32)
    for h in range(FOX_HEADS):
        base = (h // 2) * FOX_AUG + (h % 2) * FOX_SUB
        for piece in range(3):
            pk[piece * LANES + LANE_FOX_F + h, base + piece] = -1.0
            pq[piece * LANES + LANE_FOX_F + h, base + 3 + piece] = 1.0
            ck[0, base + 3 + piece] = 1.0
            cq[0, base + piece] = 1.0
    return (jnp.asarray(pk, BF16), jnp.asarray(pq, BF16), jnp.asarray(ck), jnp.asarray(cq))


def _fox_bias_lanes(f, place_ref, ones_ref):
    f = f * LOG2E
    hi = f.astype(BF16)
    r1 = f - hi.astype(F32)
    mid = r1.astype(BF16)
    lo = (r1 - mid.astype(F32)).astype(BF16)
    x = jnp.concatenate([hi, mid, lo], axis=-1)
    return (jnp.dot(x, place_ref[...], preferred_element_type=F32) + ones_ref[...]).astype(BF16)


def _fox_kernel(q_ref, k_ref, v_ref, g_ref, pk_ref, pq_ref, ck_ref, cq_ref, o_ref,
                kaug_ref, vt_ref, qaug_ref, m_ref, acc_ref, al_ref, st_ref, p_ref, *, blk, qblk):
    S = k_ref.shape[0]
    ratio = qblk // blk
    qi = pl.program_id(1)

    @pl.when(qi == 0)
    def _():
        ones = jnp.ones((FOX_VT_ROWS - FOX_HEAD_DIM, blk), BF16)
        for c in range(S // blk):
            rows = slice(c * blk, (c + 1) * blk)
            kaug_ref[rows, :] = _fox_bias_lanes(g_ref[rows, :], pk_ref, ck_ref)
            for j in range(FOX_PAIRS):
                vt = v_ref[rows, j * FOX_PAIR:(j + 1) * FOX_PAIR].T
                for hh in range(2):
                    vt_ref[c, 2 * j + hh, :FOX_HEAD_DIM, :] = vt[hh * FOX_HEAD_DIM:(hh + 1) * FOX_HEAD_DIM, :]
                    vt_ref[c, 2 * j + hh, FOX_HEAD_DIM:, :] = ones

    q_rows = pl.ds(pl.multiple_of(qi * qblk, qblk), qblk)
    qaug = _fox_bias_lanes(g_ref[q_rows, :], pq_ref, cq_ref)
    q_half = lax.broadcasted_iota(jnp.int32, (qblk, FOX_PAIR), 1) // FOX_HEAD_DIM
    aug_half = lax.broadcasted_iota(jnp.int32, (qblk, FOX_AUG), 1) // FOX_SUB
    zero = jnp.zeros((), BF16)
    scale = FOX_HEAD_DIM ** -0.5 * LOG2E
    for j in range(FOX_PAIRS):
        qs = (q_ref[:, j * FOX_PAIR:(j + 1) * FOX_PAIR].astype(F32) * scale).astype(BF16)
        qa = qaug[:, j * FOX_AUG:(j + 1) * FOX_AUG]
        both = [jnp.concatenate([jnp.where(q_half == hh, qs, zero), jnp.where(aug_half == hh, qa, zero)], axis=-1).T
                for hh in range(2)]
        qaug_ref[j] = jnp.concatenate(both, axis=-1)
    m_ref[...] = jnp.full_like(m_ref, -jnp.inf)
    acc_ref[...] = jnp.zeros_like(acc_ref)

    def score_pair(kb, slot, j, q0=0):
        ks = pl.ds(pl.multiple_of(kb * blk, blk), blk)
        kk = jnp.concatenate([k_ref[ks, j * FOX_PAIR:(j + 1) * FOX_PAIR],
                              kaug_ref[ks, j * FOX_AUG:(j + 1) * FOX_AUG]], axis=-1)
        if q0 == 0:
            st_ref[slot, j] = jnp.dot(kk, qaug_ref[j], preferred_element_type=F32)
        else:
            for hh in range(2):
                window = slice(hh * qblk + q0, (hh + 1) * qblk)
                st_ref[slot, j, :, window] = jnp.dot(kk, qaug_ref[j, :, window], preferred_element_type=F32)

    def softmax_head(slot, h, diag, q0=0):
        st = st_ref[slot, h // 2, :, (h % 2) * qblk + q0:(h % 2 + 1) * qblk]
        if diag:
            key_pos = lax.broadcasted_iota(jnp.int32, (blk, qblk - q0), 0)
            qry_pos = lax.broadcasted_iota(jnp.int32, (blk, qblk - q0), 1)
            st = jnp.where(key_pos <= qry_pos, st, -jnp.inf)
        m_row = m_ref[h:h + 1, :]
        m_old = m_row[:, q0:]
        m_new = jnp.maximum(m_old, jnp.max(st, axis=0, keepdims=True))
        alpha = jnp.exp2(m_old - m_new)
        if q0 == 0:
            al_ref[h:h + 1, :] = alpha
            m_ref[h:h + 1, :] = m_new
        else:
            al_ref[h:h + 1, :] = jnp.concatenate([jnp.ones((1, q0), F32), alpha], axis=-1)
            m_ref[h:h + 1, :] = jnp.concatenate([m_row[:, :q0], m_new], axis=-1)
        p_ref[h, :, q0:] = jnp.exp2(st - m_new).astype(BF16)

    def value_head(kb, h, q0=0):
        pv = jnp.dot(vt_ref[kb, h], p_ref[h, :, q0:], preferred_element_type=F32)
        alpha = al_ref[h:h + 1, :]
        acc_ref[h, :, q0:] = alpha[:, q0:] * acc_ref[h, :, q0:] + pv

    def advance(kb, slot, diag, q0=0, next_q0=None):
        for j in range(FOX_PAIRS):
            if next_q0 is not None:
                score_pair(kb + 1, 1 - slot, j, next_q0)
            softmax_head(slot, 2 * j, diag, q0)
            softmax_head(slot, 2 * j + 1, diag, q0)
            if j > 0:
                value_head(kb, 2 * j - 2, q0)
                value_head(kb, 2 * j - 1, q0)
        value_head(kb, FOX_HEADS - 2, q0)
        value_head(kb, FOX_HEADS - 1, q0)

    for j in range(FOX_PAIRS):
        score_pair(0, 0, j)

    def two_blocks(i, carry):
        advance(2 * i, 0, False, next_q0=0)
        advance(2 * i + 1, 1, False, next_q0=0)
        return carry

    first_diag = ratio * qi
    lax.fori_loop(0, first_diag // 2, two_blocks, 0)
    for d in range(ratio):
        advance(first_diag + d, d % 2, True, q0=d * blk, next_q0=(d + 1) * blk if d + 1 < ratio else None)

    outs = []
    for j in range(FOX_HEADS // 2):
        tops = []
        for h in (2 * j, 2 * j + 1):
            a = acc_ref[h]
            tops.append(a[:FOX_HEAD_DIM, :] / a[FOX_HEAD_DIM:FOX_HEAD_DIM + 1, :])
        outs.append(jnp.concatenate(tops, axis=0).T)
    o_ref[...] = jnp.concatenate(outs, axis=-1).astype(o_ref.dtype)


def fox_attention(z3, g3, *, blk=256, qblk=512):
    B, S, _ = z3.shape
    assert qblk % (2 * blk) == 0 and S % qblk == 0
    wq = _WIDE_OFFSETS["fox_q"] // FOX_WIDTH
    wk = _WIDE_OFFSETS["fox_k"] // FOX_WIDTH
    wv = _WIDE_OFFSETS["fox_v"] // FOX_WIDTH
    pk, pq, ck, cq = _fox_place_matrices()
    const = lambda a: pl.BlockSpec(a.shape, lambda b, i: (0,) * a.ndim)
    return pl.pallas_call(
        functools.partial(_fox_kernel, blk=blk, qblk=qblk),
        grid=(B, S // qblk),
        in_specs=[
            pl.BlockSpec((None, qblk, FOX_WIDTH), lambda b, i: (b, i, wq)),
            pl.BlockSpec((None, S, FOX_WIDTH), lambda b, i: (b, 0, wk)),
            pl.BlockSpec((None, S, FOX_WIDTH), lambda b, i: (b, 0, wv)),
            pl.BlockSpec((None, S, LANES), lambda b, i: (b, 0, 0)),
            const(pk), const(pq), const(ck), const(cq),
        ],
        out_specs=pl.BlockSpec((None, qblk, FOX_WIDTH), lambda b, i: (b, i, 0)),
        out_shape=jax.ShapeDtypeStruct((B, S, FOX_WIDTH), BF16),
        scratch_shapes=[
            pltpu.VMEM((S, FOX_PAIRS * FOX_AUG), BF16),
            pltpu.VMEM((S // blk, FOX_HEADS, FOX_VT_ROWS, blk), BF16),
            pltpu.VMEM((FOX_PAIRS, FOX_PAIR + FOX_AUG, 2 * qblk), BF16),
            pltpu.VMEM((FOX_HEADS, qblk), F32),
            pltpu.VMEM((FOX_HEADS, FOX_VT_ROWS, qblk), F32),
            pltpu.VMEM((FOX_HEADS, qblk), F32),
            pltpu.VMEM((2, FOX_PAIRS, blk, 2 * qblk), F32),
            pltpu.VMEM((FOX_HEADS, blk, qblk), BF16),
        ],
        compiler_params=_params(("parallel", "arbitrary"), 56),
        name="fox",
    )(z3, z3, z3, g3, pk, pq, ck, cq)


def _mlstm_kernel(q_ref, k_ref, v_ref, o_ref, cw_ref, gn_ref, g_ref, gt_ref, y_ref,
                  cn_ref, st_ref, halo_ref, gnb_ref, *, L):
    S = q_ref.shape[0]
    HALO = halo_ref.shape[0]
    src = lax.broadcasted_iota(jnp.int32, (L, L), 0)
    qry = lax.broadcasted_iota(jnp.int32, (L, L), 1)
    triu = src <= qry
    lane = lax.broadcasted_iota(jnp.int32, (L, LANES), 1)
    ones_col = jnp.where(lane == 0, 1.0, 0.0).astype(BF16)
    k_scale = ML_HEAD_DIM ** -0.5
    dn_t = (((1,), (1,)), ((), ()))
    dn_0 = (((0,), (0,)), ((), ()))

    cn_ref[...] = jnp.zeros_like(cn_ref)
    st_ref[...] = jnp.zeros_like(st_ref)
    halo_ref[...] = jnp.zeros_like(halo_ref)
    for h in range(ML_HEADS):
        gnb_ref[h] = jnp.broadcast_to(gn_ref[h * ML_HEAD_DIM:(h + 1) * ML_HEAD_DIM, :], (ML_HEAD_DIM, L))

    def conv_silu(x_chunk, halo, w):
        xx = jnp.concatenate([halo, x_chunk], axis=0)
        y = w[CONV_WIDTH - 1:CONV_WIDTH, :] * x_chunk
        for j in range(CONV_WIDTH - 1):
            sh = CONV_WIDTH - 1 - j
            y = y + w[j:j + 1, :] * xx[HALO - sh:HALO - sh + L, :]
        return y * jax.nn.sigmoid(y)

    def chunk(ci, _):
        rows = pl.ds(pl.multiple_of(ci * L, L), L)
        xq = q_ref[rows, :].astype(F32)
        xk = k_ref[rows, :].astype(F32)
        qa = conv_silu(xq, halo_ref[:, :ML_WIDTH], cw_ref[:, :ML_WIDTH])
        ka = conv_silu(xk, halo_ref[:, ML_WIDTH:], cw_ref[:, ML_WIDTH:]) * k_scale
        halo_ref[:, :ML_WIDTH] = xq[L - HALO:, :]
        halo_ref[:, ML_WIDTH:] = xk[L - HALO:, :]
        gcol = g_ref[rows, :]
        grow = gt_ref[ci]

        for h in range(ML_HEADS):
            cols = slice(h * ML_HEAD_DIM, (h + 1) * ML_HEAD_DIM)
            qb = qa[:, cols].astype(BF16)
            kf = ka[:, cols]
            vb = v_ref[rows, cols]
            c_col = gcol[:, LANE_ML_I + h:LANE_ML_I + h + 1] - gcol[:, LANE_ML_F + h:LANE_ML_F + h + 1]
            f_r = grow[LANE_ML_F + h:LANE_ML_F + h + 1, :]
            f_prev = st_ref[h, 0:1, 0:1]
            m_prev = st_ref[h, 0:1, 1:2]

            d_t = jnp.where(triu, c_col + f_r, -jnp.inf)
            inter = f_r - f_prev + m_prev
            m = jnp.maximum(inter, jnp.max(d_t, axis=0, keepdims=True))
            w_inter = jnp.exp(inter - m)
            p_t = lax.dot_general(kf.astype(BF16), qb, dn_t, preferred_element_type=F32) * jnp.exp(d_t - m)
            io_t = lax.dot_general(cn_ref[h].astype(BF16), qb, dn_t, preferred_element_type=F32)
            pv_t = lax.dot_general(vb, p_t.astype(BF16), dn_0, preferred_element_type=F32)
            num = w_inter * io_t[:ML_HEAD_DIM, :] + pv_t
            den = w_inter * io_t[ML_HEAD_DIM:ML_HEAD_DIM + 1, :] + jnp.sum(p_t, axis=0, keepdims=True)
            h_t = num / jnp.maximum(jnp.abs(den), jnp.exp(-m))

            m_new = m[:, L - 1:L]
            f_end = f_r[:, L - 1:L]
            decay = jnp.exp(f_end - f_prev + m_prev - m_new)
            w_s = jnp.exp(jnp.broadcast_to(c_col, (L, ML_HEAD_DIM)) + (f_end - m_new))
            kw = (w_s * kf).astype(BF16)
            v_aug = jnp.concatenate([vb, ones_col], axis=-1)
            upd = lax.dot_general(v_aug, kw, dn_0, preferred_element_type=F32)
            cn_ref[h] = decay * cn_ref[h] + upd
            st_ref[h, 0:1, 0:1] = f_end
            st_ref[h, 0:1, 1:2] = m_new

            hn_t = h_t * lax.rsqrt(jnp.mean(h_t * h_t, axis=0, keepdims=True) + EPS) * gnb_ref[h]
            y = jax.nn.sigmoid(o_ref[rows, cols].astype(F32)) * hn_t.T
            y_ref[rows, cols] = y.astype(y_ref.dtype)
        return 0

    lax.fori_loop(0, S // L, chunk, 0)


def mlstm(z3, conv_w, mlstm_g, g3, gt4, *, layer, L=CHUNK):
    B, S, _ = z3.shape
    blocks = [_WIDE_OFFSETS[n] // ML_WIDTH for n in ("ml_q", "ml_k", "ml_v", "ml_o")]
    seq_spec = lambda idx: pl.BlockSpec((None, S, ML_WIDTH), lambda b: (b, 0, idx))
    return pl.pallas_call(
        functools.partial(_mlstm_kernel, L=L),
        grid=(B,),
        in_specs=[
            seq_spec(blocks[0]), seq_spec(blocks[1]), seq_spec(blocks[2]), seq_spec(blocks[3]),
            _per_layer((CONV_WIDTH, 2 * ML_WIDTH), layer),
            _per_layer((ML_WIDTH, 1), layer),
            pl.BlockSpec((None, S, LANES), lambda b: (b, 0, 0)),
            pl.BlockSpec((None, S // L, N_GATE_LANES, L), lambda b: (b, 0, 0, 0)),
        ],
        out_specs=pl.BlockSpec((None, S, ML_WIDTH), lambda b: (b, 0, 0)),
        out_shape=jax.ShapeDtypeStruct((B, S, ML_WIDTH), BF16),
        scratch_shapes=[
            pltpu.VMEM((ML_HEADS, 2 * ML_HEAD_DIM, ML_HEAD_DIM), F32),
            pltpu.VMEM((ML_HEADS, 8, LANES), F32),
            pltpu.VMEM((ML_HALO, 2 * ML_WIDTH), F32),
            pltpu.VMEM((ML_HEADS, ML_HEAD_DIM, L), F32),
        ],
        compiler_params=_params(("parallel",), 40),
        name="mlstm",
    )(z3, z3, z3, z3, conv_w, mlstm_g, g3, gt4)


def _gelu_tanh(x):
    c0 = -2.0 * np.sqrt(2.0 / np.pi) * LOG2E
    c1 = c0 * 0.044715
    return x / (1.0 + jnp.exp2(x * (c0 + c1 * (x * x))))


def _gmlp_kernel(u_ref, v_ref, gn_ref, ws_ref, bst_ref, y_ref):
    rows = u_ref.shape[0]
    u = _gelu_tanh(u_ref[...].astype(F32))
    v = _gelu_tanh(v_ref[...].astype(F32))
    mu = jnp.mean(v, axis=-1, keepdims=True)
    vc = v - mu
    vn = (vc * lax.rsqrt(jnp.mean(vc * vc, axis=-1, keepdims=True) + EPS) * gn_ref[...]).astype(BF16)
    r = lax.broadcasted_iota(jnp.int32, (G_SPAN, G_SPAN), 0) // CHUNK
    c = lax.broadcasted_iota(jnp.int32, (G_SPAN, G_SPAN), 1) // CHUNK
    mask = r >= c
    for g in range(G_GROUPS):
        cols = slice(g * G_GROUP_DIM, (g + 1) * G_GROUP_DIM)
        w = jnp.where(mask, ws_ref[g], 0.0).astype(BF16)
        bias = bst_ref[:, g:g + 1]
        for s in range(rows // G_SPAN):
            rs = slice(s * G_SPAN, (s + 1) * G_SPAN)
            mixed = jnp.dot(w, vn[rs, cols], preferred_element_type=F32) + bias
            y_ref[rs, cols] = (u[rs, cols] * mixed).astype(y_ref.dtype)


def gmlp(z3, gmlp_g, ws, bs_t, *, layer, rows=512):
    B, S, _ = z3.shape
    rows = min(rows, S)
    bu = _WIDE_OFFSETS["g_u"] // G_WIDTH
    bv = _WIDE_OFFSETS["g_v"] // G_WIDTH
    return pl.pallas_call(
        _gmlp_kernel,
        grid=(B, S // rows),
        in_specs=[
            pl.BlockSpec((None, rows, G_WIDTH), lambda b, i: (b, i, bu)),
            pl.BlockSpec((None, rows, G_WIDTH), lambda b, i: (b, i, bv)),
            _per_layer((1, G_WIDTH), layer),
            _per_layer((G_GROUPS, G_SPAN, G_SPAN), layer),
            _per_layer((G_SPAN, G_GROUPS), layer),
        ],
        out_specs=pl.BlockSpec((None, rows, G_WIDTH), lambda b, i: (b, i, 0)),
        out_shape=jax.ShapeDtypeStruct((B, S, G_WIDTH), BF16),
        compiler_params=_params(("parallel", "parallel"), 16),
        name="gmlp",
    )(z3, z3, gmlp_g, ws, bs_t)


def _merge_kernel(x_ref, gate_ref, ya_ref, yb_ref, yc_ref, wb_ref, wo_ref, gpost_ref, o_ref, *, sub):
    for r in range(x_ref.shape[0] // sub):
        rows = slice(r * sub, (r + 1) * sub)
        merged = None
        for n, y_ref in enumerate((ya_ref, yb_ref, yc_ref)):
            br = jnp.dot(y_ref[rows, :], wb_ref[n], preferred_element_type=F32)
            gt = jax.nn.sigmoid(gate_ref[rows, n * D_MODEL:(n + 1) * D_MODEL].astype(F32))
            merged = gt * br if merged is None else merged + gt * br
        y = jnp.dot(merged.astype(BF16), wo_ref[...], preferred_element_type=F32)
        o_ref[rows, :] = x_ref[rows, :] + _rms(y, gpost_ref[...])


def merge(x2, z2, y_fox, y_ml, y_g, w_branch, w_out, gains, *, layer, gain, tm=1024, sub=512):
    T, D = x2.shape
    tm = min(tm, T)
    row = lambda w: pl.BlockSpec((tm, w), lambda i: (i, 0))
    return pl.pallas_call(
        functools.partial(_merge_kernel, sub=min(sub, tm)),
        grid=(T // tm,),
        in_specs=[
            row(D),
            row(N_BRANCH * D),
            row(FOX_WIDTH), row(ML_WIDTH), row(G_WIDTH),
            _per_layer((N_BRANCH, FOX_WIDTH, D), layer),
            _per_layer((D, D), layer),
            _per_layer((1, D), gain),
        ],
        out_specs=row(D),
        out_shape=jax.ShapeDtypeStruct((T, D), F32),
        compiler_params=_params(("parallel",), 48),
        name="merge",
    )(x2, z2, y_fox, y_ml, y_g, w_branch, w_out, gains)


def _norm_matmul_kernel(x_ref, g_ref, w_ref, o_ref):
    h = _rms(x_ref[...], g_ref[...]).astype(BF16)
    o_ref[...] = jnp.dot(h, w_ref[...], preferred_element_type=F32).astype(o_ref.dtype)


def norm_matmul(x2, gains, w, *, layer, gain, tm=512):
    T, D = x2.shape
    N = w.shape[-1]
    tm = min(tm, T)
    return pl.pallas_call(
        _norm_matmul_kernel,
        grid=(T // tm,),
        in_specs=[
            pl.BlockSpec((tm, D), lambda i: (i, 0)),
            _per_layer((1, D), gain),
            _per_layer((D, N), layer),
        ],
        out_specs=pl.BlockSpec((tm, N), lambda i: (i, 0)),
        out_shape=jax.ShapeDtypeStruct((T, N), BF16),
        compiler_params=_params(("parallel",), 40),
        name="mem_kv",
    )(x2, gains, w)


def _xattn_kernel(x_ref, kv_ref, wq_ref, wo_ref, gpre_ref, gpost_ref, o_ref, *, sub):
    dn = (((1,), (1,)), ((), ()))
    for r in range(x_ref.shape[0] // sub):
        rows = slice(r * sub, (r + 1) * sub)
        x = x_ref[rows, :]
        h = _rms(x, gpre_ref[...]).astype(BF16)
        q = jnp.dot(h, wq_ref[...], preferred_element_type=F32).astype(BF16)
        outs = []
        for hd in range(X_HEADS):
            cols = slice(hd * X_HEAD_DIM, (hd + 1) * X_HEAD_DIM)
            k = kv_ref[:, cols]
            v = kv_ref[:, D_MODEL + hd * X_HEAD_DIM:D_MODEL + (hd + 1) * X_HEAD_DIM]
            s = lax.dot_general(q[:, cols], k, dn, preferred_element_type=F32) * (X_HEAD_DIM ** -0.5)
            s = s - jnp.max(s, axis=-1, keepdims=True)
            e = jnp.exp(s)
            p = e / jnp.sum(e, axis=-1, keepdims=True)
            outs.append(jnp.dot(p.astype(BF16), v, preferred_element_type=F32).astype(BF16))
        o = jnp.concatenate(outs, axis=-1)
        y = jnp.dot(o, wo_ref[...], preferred_element_type=F32)
        o_ref[rows, :] = x + _rms(y, gpost_ref[...])


def xattn(x3, kv3, w_q, w_o, gains, *, layer, gain_pre, gain_post, tm=2048, sub=512):
    B, S, D = x3.shape
    M = kv3.shape[1]
    tm = min(tm, S)
    return pl.pallas_call(
        functools.partial(_xattn_kernel, sub=min(sub, tm)),
        grid=(B, S // tm),
        in_specs=[
            pl.BlockSpec((None, tm, D), lambda b, i: (b, i, 0)),
            pl.BlockSpec((None, M, 2 * D), lambda b, i: (b, 0, 0)),
            _per_layer((D, D), layer), _per_layer((D, D), layer),
            _per_layer((1, D), gain_pre), _per_layer((1, D), gain_post),
        ],
        out_specs=pl.BlockSpec((None, tm, D), lambda b, i: (b, i, 0)),
        out_shape=jax.ShapeDtypeStruct((B, S, D), F32),
        compiler_params=_params(("parallel", "arbitrary"), 56),
        name="xattn",
    )(x3, kv3, w_q, w_o, gains, gains)


def _ffn_kernel(x_ref, w1_ref, w2_ref, gpre_ref, gpost_ref, o_ref, *, ff_chunk, sub):
    for r in range(x_ref.shape[0] // sub):
        rows = slice(r * sub, (r + 1) * sub)
        x = x_ref[rows, :]
        h = _rms(x, gpre_ref[...]).astype(BF16)
        acc = None
        for c in range(w1_ref.shape[1] // ff_chunk):
            cs = slice(c * ff_chunk, (c + 1) * ff_chunk)
            a = jnp.dot(h, w1_ref[:, cs], preferred_element_type=F32)
            a = jnp.square(jnp.maximum(a, 0.0)).astype(BF16)
            part = jnp.dot(a, w2_ref[cs, :], preferred_element_type=F32)
            acc = part if acc is None else acc + part
        o_ref[rows, :] = x + _rms(acc, gpost_ref[...])


def ffn(x2, w1, w2, gains, *, layer, gain_pre, gain_post, tm=1024, sub=512, ff_chunk=1024):
    T, D = x2.shape
    FF = w1.shape[-1]
    tm = min(tm, T)
    return pl.pallas_call(
        functools.partial(_ffn_kernel, ff_chunk=ff_chunk, sub=min(sub, tm)),
        grid=(T // tm,),
        in_specs=[
            pl.BlockSpec((tm, D), lambda i: (i, 0)),
            _per_layer((D, FF), layer),
            _per_layer((FF, D), layer),
            _per_layer((1, D), gain_pre),
            _per_layer((1, D), gain_post),
        ],
        out_specs=pl.BlockSpec((tm, D), lambda i: (i, 0)),
        out_shape=jax.ShapeDtypeStruct((T, D), F32),
        compiler_params=_params(("parallel",), 56),
        name="ffn",
    )(x2, w1, w2, gains, gains)


def _regroup_kernel(wt_ref, wide_ref, narrow_ref):
    for name in _WIDE_ORDER:
        src, size = _IN_OFFSETS[name]
        dst = _WIDE_OFFSETS[name]
        wide_ref[dst:dst + size, :] = wt_ref[src:src + size, :].astype(BF16)
    narrow_ref[...] = jnp.zeros_like(narrow_ref)
    dst = 0
    for name in _NARROW_ORDER:
        src, size = _IN_OFFSETS[name]
        narrow_ref[dst:dst + size, :] = wt_ref[src:src + size, :].astype(BF16)
        dst += size


def _regroup_in_proj(w_in, b_in, *, cols_per_step=256):
    depth, D, d_in = w_in.shape
    w_in_t = jnp.swapaxes(w_in, 1, 2)
    w_wide, w_narrow = pl.pallas_call(
        _regroup_kernel,
        grid=(depth, D // cols_per_step),
        in_specs=[pl.BlockSpec((None, d_in, cols_per_step), lambda l, i: (l, 0, i))],
        out_specs=[pl.BlockSpec((None, N_WIDE, cols_per_step), lambda l, i: (l, 0, i)),
                   pl.BlockSpec((None, LANES, cols_per_step), lambda l, i: (l, 0, i))],
        out_shape=[jax.ShapeDtypeStruct((depth, N_WIDE, D), BF16),
                   jax.ShapeDtypeStruct((depth, LANES, D), BF16)],
        compiler_params=_params(("parallel", "parallel"), 48),
        name="regroup",
    )(w_in_t)

    def cols(names):
        return [slice(_IN_OFFSETS[n][0], _IN_OFFSETS[n][0] + _IN_OFFSETS[n][1]) for n in names]
    b_wide = jnp.concatenate([b_in[..., s] for s in cols(_WIDE_ORDER)], axis=-1)[:, None, :]
    pad = LANES - N_GATE_LANES
    b_narrow = jnp.pad(jnp.concatenate([b_in[..., s] for s in cols(_NARROW_ORDER)], axis=-1),
                       ((0, 0), (0, pad)))[:, None, :]
    return w_wide, b_wide, w_narrow, b_narrow


def _prepare_params(norms, w_in, b_in, conv_w, mlstm_norm, gmlp_norm, gmlp_ws, gmlp_bs,
                    w_branch, w_out, w_xq, w_xkv, w_xo, w_ff1, w_ff2):
    depth = norms.shape[0]
    w_wide, b_wide, w_narrow, b_narrow = _regroup_in_proj(w_in, b_in)
    return dict(
        gains=norms.reshape(depth * N_NORMS, 1, D_MODEL),
        w_wide=w_wide, b_wide=b_wide, w_narrow=w_narrow, b_narrow=b_narrow,
        conv_w=conv_w, mlstm_g=mlstm_norm[:, :, None], gmlp_g=gmlp_norm[:, None, :],
        gmlp_ws=gmlp_ws, gmlp_bs_t=gmlp_bs.transpose(0, 2, 1),
        w_branch=w_branch.astype(BF16), w_out=w_out.astype(BF16),
        w_xq=w_xq.astype(BF16), w_xkv=w_xkv.astype(BF16), w_xo=w_xo.astype(BF16),
        w_ff1=w_ff1.astype(BF16), w_ff2=w_ff2.astype(BF16),
    )


def _layer(x3, mem2, p, layer, *, fox_blk, ml_chunk):
    B, S, D = x3.shape
    T = B * S
    M = mem2.shape[0] // B
    gain = lambda idx: layer * N_NORMS + idx
    x2 = x3.reshape(T, D)

    z2, zn2 = in_proj(x2, p["gains"], p["w_wide"], p["b_wide"], p["w_narrow"], p["b_narrow"],
                      layer=layer, gain=gain(NORM_MIX_PRE))
    z3 = z2.reshape(B, S, N_WIDE)
    g3, gt4 = gates(zn2.reshape(B, S, LANES), L=ml_chunk)

    y_fox = fox_attention(z3, g3, blk=fox_blk, qblk=min(2 * fox_blk, S))
    y_ml = mlstm(z3, p["conv_w"], p["mlstm_g"], g3, gt4, layer=layer, L=ml_chunk)
    y_g = gmlp(z3, p["gmlp_g"], p["gmlp_ws"], p["gmlp_bs_t"], layer=layer)
    x2 = merge(x2, z2, y_fox.reshape(T, FOX_WIDTH), y_ml.reshape(T, ML_WIDTH), y_g.reshape(T, G_WIDTH),
               p["w_branch"], p["w_out"], p["gains"], layer=layer, gain=gain(NORM_MIX_POST))

    kv = norm_matmul(mem2, p["gains"], p["w_xkv"], layer=layer, gain=gain(NORM_MEM))
    x3 = xattn(x2.reshape(B, S, D), kv.reshape(B, M, 2 * D), p["w_xq"], p["w_xo"], p["gains"],
               layer=layer, gain_pre=gain(NORM_X_PRE), gain_post=gain(NORM_X_POST))

    x2 = ffn(x3.reshape(T, D), p["w_ff1"], p["w_ff2"], p["gains"],
             layer=layer, gain_pre=gain(NORM_FF_PRE), gain_post=gain(NORM_FF_POST))
    return x2.reshape(B, S, D)


def kernel(x, mem, norms, w_in, b_in, conv_w, mlstm_norm, gmlp_norm, gmlp_ws, gmlp_bs, w_branch, w_out, w_xq, w_xkv, w_xo, w_ff1, w_ff2):
    B, M, D = mem.shape
    S = x.shape[1]
    mem2 = mem.reshape(B * M, D)
    p = _prepare_params(norms, w_in, b_in, conv_w, mlstm_norm, gmlp_norm, gmlp_ws, gmlp_bs,
                        w_branch, w_out, w_xq, w_xkv, w_xo, w_ff1, w_ff2)
    for layer in range(norms.shape[0]):
        x = _layer(x, mem2, p, layer, fox_blk=min(256, S), ml_chunk=min(ML_CHUNK, S))
    return x
```

```python
import functools

import jax
import jax.numpy as jnp
import numpy as np
from jax import lax
from jax.experimental import pallas as pl
from jax.experimental.pallas import tpu as pltpu

F32 = jnp.float32
BF16 = jnp.bfloat16

EPS = 1e-6
D_MODEL = 1024
CHUNK = 64
FOX_HEADS = 8
FOX_HEAD_DIM = 64
FOX_WIDTH = FOX_HEADS * FOX_HEAD_DIM
ML_HEADS = 4
ML_HEAD_DIM = 128
ML_WIDTH = ML_HEADS * ML_HEAD_DIM
ML_CHUNK = 128
ML_HALO = 8
CONV_WIDTH = 4
G_GROUPS = 4
G_GROUP_DIM = 128
G_WIDTH = G_GROUPS * G_GROUP_DIM
G_SPAN = 128
N_BRANCH = 3
X_HEADS = 4
X_HEAD_DIM = D_MODEL // X_HEADS
D_FF = 4 * D_MODEL
NORM_MIX_PRE, NORM_MIX_POST, NORM_X_PRE, NORM_X_POST, NORM_MEM, NORM_FF_PRE, NORM_FF_POST = range(7)
N_NORMS = 7

LOG2E = 1.4426950408889634
LANES = 128
MIB = 1024 * 1024

_IN_SPLITS = (
    ("fox_q", FOX_WIDTH), ("fox_k", FOX_WIDTH), ("fox_v", FOX_WIDTH), ("fox_f", FOX_HEADS),
    ("ml_q", ML_WIDTH), ("ml_k", ML_WIDTH), ("ml_v", ML_WIDTH),
    ("ml_i", ML_HEADS), ("ml_f", ML_HEADS), ("ml_o", ML_WIDTH),
    ("g_u", G_WIDTH), ("g_v", G_WIDTH),
    ("gate", N_BRANCH * D_MODEL),
)
_IN_OFFSETS = {}
_off = 0
for _name, _size in _IN_SPLITS:
    _IN_OFFSETS[_name] = (_off, _size)
    _off += _size

_WIDE_ORDER = ("gate", "fox_q", "fox_k", "fox_v", "ml_q", "ml_k", "ml_v", "ml_o", "g_u", "g_v")
_WIDE_OFFSETS = {}
_off = 0
for _name in _WIDE_ORDER:
    _WIDE_OFFSETS[_name] = _off
    _off += _IN_OFFSETS[_name][1]
N_WIDE = _off
_NARROW_ORDER = ("fox_f", "ml_i", "ml_f")
LANE_FOX_F = 0
LANE_ML_I = FOX_HEADS
LANE_ML_F = FOX_HEADS + ML_HEADS
N_GATE_LANES = FOX_HEADS + 2 * ML_HEADS


def _rms(x, g):
    return x * lax.rsqrt(jnp.mean(x * x, axis=-1, keepdims=True) + EPS) * g


def _log_sigmoid(x):
    return jnp.minimum(x, 0.0) - jnp.log1p(jnp.exp(-jnp.abs(x)))


def _params(semantics, vmem_mib):
    return pltpu.CompilerParams(dimension_semantics=semantics, vmem_limit_bytes=vmem_mib * MIB)


def _per_layer(block, layer, tail=None):
    block = tuple(block)
    if tail is None:
        return pl.BlockSpec((None,) + block, lambda *ids: (layer,) + (0,) * len(block),
                            pipeline_mode=pl.Buffered(1))
    return pl.BlockSpec((None,) + block, lambda *ids: (layer,) + tuple(tail(*ids)))


def _in_proj_kernel(x_ref, g_ref, wt_ref, b_ref, wnt_ref, bn_ref, z_ref, zn_ref, h_ref, *, sub):
    j = pl.program_id(1)
    nt = (((1,), (1,)), ((), ()))

    @pl.when(j == 0)
    def _():
        for r in range(x_ref.shape[0] // sub):
            rows = slice(r * sub, (r + 1) * sub)
            h = _rms(x_ref[rows, :], g_ref[...]).astype(BF16)
            h_ref[rows, :] = h
            zn_ref[rows, :] = lax.dot_general(h, wnt_ref[...], nt, preferred_element_type=F32) + bn_ref[...]
            z = lax.dot_general(h, wt_ref[...], nt, preferred_element_type=F32) + b_ref[...]
            z_ref[rows, :] = z.astype(z_ref.dtype)

    @pl.when(j != 0)
    def _():
        z = lax.dot_general(h_ref[...], wt_ref[...], nt, preferred_element_type=F32) + b_ref[...]
        z_ref[...] = z.astype(z_ref.dtype)


def in_proj(x2, gains, w_wide, b_wide, w_narrow, b_narrow, *, layer, gain, tm=2048, tn=1536, sub=512):
    T, D = x2.shape
    N = w_wide.shape[1]
    tm = min(tm, T)
    return pl.pallas_call(
        functools.partial(_in_proj_kernel, sub=min(sub, tm)),
        grid=(T // tm, N // tn),
        in_specs=[
            pl.BlockSpec((tm, D), lambda i, j: (i, 0)),
            _per_layer((1, D), gain),
            _per_layer((tn, D), layer, lambda i, j: (j, 0)),
            _per_layer((1, tn), layer, lambda i, j: (0, j)),
            _per_layer((LANES, D), layer),
            _per_layer((1, LANES), layer),
        ],
        out_specs=[
            pl.BlockSpec((tm, tn), lambda i, j: (i, j)),
            pl.BlockSpec((tm, LANES), lambda i, j: (i, 0)),
        ],
        out_shape=[
            jax.ShapeDtypeStruct((T, N), BF16),
            jax.ShapeDtypeStruct((T, LANES), F32),
        ],
        scratch_shapes=[pltpu.VMEM((tm, D), BF16)],
        compiler_params=_params(("parallel", "arbitrary"), 56),
        name="in_proj",
    )(x2, gains, w_wide, b_wide, w_narrow, b_narrow)


def _gates_kernel(zn_ref, g_ref, gt_ref, *, rows):
    S = zn_ref.shape[0]
    L = gt_ref.shape[-1]
    lane = lax.broadcasted_iota(jnp.int32, (rows, LANES), 1)
    keep_raw = jnp.logical_and(lane >= LANE_ML_I, lane < LANE_ML_F)
    r = lax.broadcasted_iota(jnp.int32, (rows, rows), 0)
    c = lax.broadcasted_iota(jnp.int32, (rows, rows), 1)
    tril = jnp.where(r >= c, 1.0, 0.0).astype(F32)

    offset = jnp.zeros((1, LANES), F32)
    for i in range(S // rows):
        sl = slice(i * rows, (i + 1) * rows)
        z = zn_ref[sl, :]
        ls = _log_sigmoid(z)
        cs = jnp.dot(tril, ls, precision=lax.Precision.HIGHEST, preferred_element_type=F32) + offset
        out = jnp.where(keep_raw, z, cs)
        g_ref[sl, :] = out
        out_t = out.T
        for k in range(rows // L):
            gt_ref[i * (rows // L) + k] = out_t[:N_GATE_LANES, k * L:(k + 1) * L]
        offset = cs[rows - 1:rows, :]


def gates(zn3, *, L, rows=256):
    B, S, _ = zn3.shape
    rows = min(rows, S)
    assert rows % L == 0
    return pl.pallas_call(
        functools.partial(_gates_kernel, rows=rows),
        grid=(B,),
        in_specs=[pl.BlockSpec((None, S, LANES), lambda b: (b, 0, 0))],
        out_specs=[pl.BlockSpec((None, S, LANES), lambda b: (b, 0, 0)),
                   pl.BlockSpec((None, S // L, N_GATE_LANES, L), lambda b: (b, 0, 0, 0))],
        out_shape=[jax.ShapeDtypeStruct((B, S, LANES), F32),
                   jax.ShapeDtypeStruct((B, S // L, N_GATE_LANES, L), F32)],
        compiler_params=_params(("parallel",), 16),
        name="gates",
    )(zn3)


FOX_AUG = LANES
FOX_SUB = 8
FOX_PAIR = 2 * FOX_HEAD_DIM
FOX_PAIRS = FOX_HEADS // 2
FOX_VT_ROWS = FOX_HEAD_DIM + 16


def _fox_place_matrices():
    pk = np.zeros((3 * LANES, FOX_PAIRS * FOX_AUG), np.float32)
    pq = np.zeros((3 * LANES, FOX_PAIRS * FOX_AUG), np.float32)
    ck = np.zeros((1, FOX_PAIRS * FOX_AUG), np.float32)
    cq = np.zeros((1, FOX_PAIRS * FOX_AUG), np.float32)
    for h in range(FOX_HEADS):
        base = (h // 2) * FOX_AUG + (h % 2) * FOX_SUB
        for piece in range(3):
            pk[piece * LANES + LANE_FOX_F + h, base + piece] = -1.0
            pq[piece * LANES + LANE_FOX_F + h, base + 3 + piece] = 1.0
            ck[0, base + 3 + piece] = 1.0
            cq[0, base + piece] = 1.0
    return (jnp.asarray(pk, BF16), jnp.asarray(pq, BF16), jnp.asarray(ck), jnp.asarray(cq))


def _fox_bias_lanes(f, place_ref, ones_ref):
    f = f * LOG2E
    hi = f.astype(BF16)
    r1 = f - hi.astype(F32)
    mid = r1.astype(BF16)
    lo = (r1 - mid.astype(F32)).astype(BF16)
    x = jnp.concatenate([hi, mid, lo], axis=-1)
    return (jnp.dot(x, place_ref[...], preferred_element_type=F32) + ones_ref[...]).astype(BF16)


def _fox_kernel(q_ref, k_ref, v_ref, g_ref, pk_ref, pq_ref, ck_ref, cq_ref, o_ref,
                kaug_ref, vt_ref, qaug_ref, m_ref, acc_ref, al_ref, st_ref, p_ref, *, blk, qblk):
    S = k_ref.shape[0]
    ratio = qblk // blk
    qi = pl.program_id(1)

    @pl.when(qi == 0)
    def _():
        ones = jnp.ones((FOX_VT_ROWS - FOX_HEAD_DIM, blk), BF16)
        for c in range(S // blk):
            rows = slice(c * blk, (c + 1) * blk)
            kaug_ref[rows, :] = _fox_bias_lanes(g_ref[rows, :], pk_ref, ck_ref)
            for j in range(FOX_PAIRS):
                vt = v_ref[rows, j * FOX_PAIR:(j + 1) * FOX_PAIR].T
                for hh in range(2):
                    vt_ref[c, 2 * j + hh, :FOX_HEAD_DIM, :] = vt[hh * FOX_HEAD_DIM:(hh + 1) * FOX_HEAD_DIM, :]
                    vt_ref[c, 2 * j + hh, FOX_HEAD_DIM:, :] = ones

    q_rows = pl.ds(pl.multiple_of(qi * qblk, qblk), qblk)
    qaug = _fox_bias_lanes(g_ref[q_rows, :], pq_ref, cq_ref)
    q_half = lax.broadcasted_iota(jnp.int32, (qblk, FOX_PAIR), 1) // FOX_HEAD_DIM
    aug_half = lax.broadcasted_iota(jnp.int32, (qblk, FOX_AUG), 1) // FOX_SUB
    zero = jnp.zeros((), BF16)
    scale = FOX_HEAD_DIM ** -0.5 * LOG2E
    for j in range(FOX_PAIRS):
        qs = (q_ref[:, j * FOX_PAIR:(j + 1) * FOX_PAIR].astype(F32) * scale).astype(BF16)
        qa = qaug[:, j * FOX_AUG:(j + 1) * FOX_AUG]
        both = [jnp.concatenate([jnp.where(q_half == hh, qs, zero), jnp.where(aug_half == hh, qa, zero)], axis=-1).T
                for hh in range(2)]
        qaug_ref[j] = jnp.concatenate(both, axis=-1)
    m_ref[...] = jnp.full_like(m_ref, -jnp.inf)
    acc_ref[...] = jnp.zeros_like(acc_ref)

    def score_pair(kb, slot, j, q0=0):
        ks = pl.ds(pl.multiple_of(kb * blk, blk), blk)
        kk = jnp.concatenate([k_ref[ks, j * FOX_PAIR:(j + 1) * FOX_PAIR],
                              kaug_ref[ks, j * FOX_AUG:(j + 1) * FOX_AUG]], axis=-1)
        if q0 == 0:
            st_ref[slot, j] = jnp.dot(kk, qaug_ref[j], preferred_element_type=F32)
        else:
            for hh in range(2):
                window = slice(hh * qblk + q0, (hh + 1) * qblk)
                st_ref[slot, j, :, window] = jnp.dot(kk, qaug_ref[j, :, window], preferred_element_type=F32)

    def softmax_head(slot, h, diag, q0=0):
        st = st_ref[slot, h // 2, :, (h % 2) * qblk + q0:(h % 2 + 1) * qblk]
        if diag:
            key_pos = lax.broadcasted_iota(jnp.int32, (blk, qblk - q0), 0)
            qry_pos = lax.broadcasted_iota(jnp.int32, (blk, qblk - q0), 1)
            st = jnp.where(key_pos <= qry_pos, st, -jnp.inf)
        m_row = m_ref[h:h + 1, :]
        m_old = m_row[:, q0:]
        m_new = jnp.maximum(m_old, jnp.max(st, axis=0, keepdims=True))
        alpha = jnp.exp2(m_old - m_new)
        if q0 == 0:
            al_ref[h:h + 1, :] = alpha
            m_ref[h:h + 1, :] = m_new
        else:
            al_ref[h:h + 1, :] = jnp.concatenate([jnp.ones((1, q0), F32), alpha], axis=-1)
            m_ref[h:h + 1, :] = jnp.concatenate([m_row[:, :q0], m_new], axis=-1)
        p_ref[h, :, q0:] = jnp.exp2(st - m_new).astype(BF16)

    def value_head(kb, h, q0=0):
        pv = jnp.dot(vt_ref[kb, h], p_ref[h, :, q0:], preferred_element_type=F32)
        alpha = al_ref[h:h + 1, :]
        acc_ref[h, :, q0:] = alpha[:, q0:] * acc_ref[h, :, q0:] + pv

    def advance(kb, slot, diag, q0=0, next_q0=None):
        for j in range(FOX_PAIRS):
            if next_q0 is not None:
                score_pair(kb + 1, 1 - slot, j, next_q0)
            softmax_head(slot, 2 * j, diag, q0)
            softmax_head(slot, 2 * j + 1, diag, q0)
            if j > 0:
                value_head(kb, 2 * j - 2, q0)
                value_head(kb, 2 * j - 1, q0)
        value_head(kb, FOX_HEADS - 2, q0)
        value_head(kb, FOX_HEADS - 1, q0)

    for j in range(FOX_PAIRS):
        score_pair(0, 0, j)

    def two_blocks(i, carry):
        advance(2 * i, 0, False, next_q0=0)
        advance(2 * i + 1, 1, False, next_q0=0)
        return carry

    first_diag = ratio * qi
    lax.fori_loop(0, first_diag // 2, two_blocks, 0)
    for d in range(ratio):
        advance(first_diag + d, d % 2, True, q0=d * blk, next_q0=(d + 1) * blk if d + 1 < ratio else None)

    outs = []
    for j in range(FOX_HEADS // 2):
        tops = []
        for h in (2 * j, 2 * j + 1):
            a = acc_ref[h]
            tops.append(a[:FOX_HEAD_DIM, :] / a[FOX_HEAD_DIM:FOX_HEAD_DIM + 1, :])
        outs.append(jnp.concatenate(tops, axis=0).T)
    o_ref[...] = jnp.concatenate(outs, axis=-1).astype(o_ref.dtype)


def fox_attention(z3, g3, *, blk=256, qblk=512):
    B, S, _ = z3.shape
    assert qblk % (2 * blk) == 0 and S % qblk == 0
    wq = _WIDE_OFFSETS["fox_q"] // FOX_WIDTH
    wk = _WIDE_OFFSETS["fox_k"] // FOX_WIDTH
    wv = _WIDE_OFFSETS["fox_v"] // FOX_WIDTH
    pk, pq, ck, cq = _fox_place_matrices()
    const = lambda a: pl.BlockSpec(a.shape, lambda b, i: (0,) * a.ndim)
    return pl.pallas_call(
        functools.partial(_fox_kernel, blk=blk, qblk=qblk),
        grid=(B, S // qblk),
        in_specs=[
            pl.BlockSpec((None, qblk, FOX_WIDTH), lambda b, i: (b, i, wq)),
            pl.BlockSpec((None, S, FOX_WIDTH), lambda b, i: (b, 0, wk)),
            pl.BlockSpec((None, S, FOX_WIDTH), lambda b, i: (b, 0, wv)),
            pl.BlockSpec((None, S, LANES), lambda b, i: (b, 0, 0)),
            const(pk), const(pq), const(ck), const(cq),
        ],
        out_specs=pl.BlockSpec((None, qblk, FOX_WIDTH), lambda b, i: (b, i, 0)),
        out_shape=jax.ShapeDtypeStruct((B, S, FOX_WIDTH), BF16),
        scratch_shapes=[
            pltpu.VMEM((S, FOX_PAIRS * FOX_AUG), BF16),
            pltpu.VMEM((S // blk, FOX_HEADS, FOX_VT_ROWS, blk), BF16),
            pltpu.VMEM((FOX_PAIRS, FOX_PAIR + FOX_AUG, 2 * qblk), BF16),
            pltpu.VMEM((FOX_HEADS, qblk), F32),
            pltpu.VMEM((FOX_HEADS, FOX_VT_ROWS, qblk), F32),
            pltpu.VMEM((FOX_HEADS, qblk), F32),
            pltpu.VMEM((2, FOX_PAIRS, blk, 2 * qblk), F32),
            pltpu.VMEM((FOX_HEADS, blk, qblk), BF16),
        ],
        compiler_params=_params(("parallel", "arbitrary"), 56),
        name="fox",
    )(z3, z3, z3, g3, pk, pq, ck, cq)


def _mlstm_kernel(q_ref, k_ref, v_ref, o_ref, cw_ref, gn_ref, g_ref, gt_ref, y_ref,
                  cn_ref, st_ref, halo_ref, gnb_ref, *, L):
    S = q_ref.shape[0]
    HALO = halo_ref.shape[0]
    src = lax.broadcasted_iota(jnp.int32, (L, L), 0)
    qry = lax.broadcasted_iota(jnp.int32, (L, L), 1)
    triu = src <= qry
    lane = lax.broadcasted_iota(jnp.int32, (L, LANES), 1)
    ones_col = jnp.where(lane == 0, 1.0, 0.0).astype(BF16)
    k_scale = ML_HEAD_DIM ** -0.5
    dn_t = (((1,), (1,)), ((), ()))
    dn_0 = (((0,), (0,)), ((), ()))

    cn_ref[...] = jnp.zeros_like(cn_ref)
    st_ref[...] = jnp.zeros_like(st_ref)
    halo_ref[...] = jnp.zeros_like(halo_ref)
    for h in range(ML_HEADS):
        gnb_ref[h] = jnp.broadcast_to(gn_ref[h * ML_HEAD_DIM:(h + 1) * ML_HEAD_DIM, :], (ML_HEAD_DIM, L))

    def conv_silu(x_chunk, halo, w):
        xx = jnp.concatenate([halo, x_chunk], axis=0)
        y = w[CONV_WIDTH - 1:CONV_WIDTH, :] * x_chunk
        for j in range(CONV_WIDTH - 1):
            sh = CONV_WIDTH - 1 - j
            y = y + w[j:j + 1, :] * xx[HALO - sh:HALO - sh + L, :]
        return y * jax.nn.sigmoid(y)

    def chunk(ci, _):
        rows = pl.ds(pl.multiple_of(ci * L, L), L)
        xq = q_ref[rows, :].astype(F32)
        xk = k_ref[rows, :].astype(F32)
        qa = conv_silu(xq, halo_ref[:, :ML_WIDTH], cw_ref[:, :ML_WIDTH])
        ka = conv_silu(xk, halo_ref[:, ML_WIDTH:], cw_ref[:, ML_WIDTH:]) * k_scale
        halo_ref[:, :ML_WIDTH] = xq[L - HALO:, :]
        halo_ref[:, ML_WIDTH:] = xk[L - HALO:, :]
        gcol = g_ref[rows, :]
        grow = gt_ref[ci]

        for h in range(ML_HEADS):
            cols = slice(h * ML_HEAD_DIM, (h + 1) * ML_HEAD_DIM)
            qb = qa[:, cols].astype(BF16)
            kf = ka[:, cols]
            vb = v_ref[rows, cols]
            c_col = gcol[:, LANE_ML_I + h:LANE_ML_I + h + 1] - gcol[:, LANE_ML_F + h:LANE_ML_F + h + 1]
            f_r = grow[LANE_ML_F + h:LANE_ML_F + h + 1, :]
            f_prev = st_ref[h, 0:1, 0:1]
            m_prev = st_ref[h, 0:1, 1:2]

            d_t = jnp.where(triu, c_col + f_r, -jnp.inf)
            inter = f_r - f_prev + m_prev
            m = jnp.maximum(inter, jnp.max(d_t, axis=0, keepdims=True))
            w_inter = jnp.exp(inter - m)
            p_t = lax.dot_general(kf.astype(BF16), qb, dn_t, preferred_element_type=F32) * jnp.exp(d_t - m)
            io_t = lax.dot_general(cn_ref[h].astype(BF16), qb, dn_t, preferred_element_type=F32)
            pv_t = lax.dot_general(vb, p_t.astype(BF16), dn_0, preferred_element_type=F32)
            num = w_inter * io_t[:ML_HEAD_DIM, :] + pv_t
            den = w_inter * io_t[ML_HEAD_DIM:ML_HEAD_DIM + 1, :] + jnp.sum(p_t, axis=0, keepdims=True)
            h_t = num / jnp.maximum(jnp.abs(den), jnp.exp(-m))

            m_new = m[:, L - 1:L]
            f_end = f_r[:, L - 1:L]
            decay = jnp.exp(f_end - f_prev + m_prev - m_new)
            w_s = jnp.exp(jnp.broadcast_to(c_col, (L, ML_HEAD_DIM)) + (f_end - m_new))
            kw = (w_s * kf).astype(BF16)
            v_aug = jnp.concatenate([vb, ones_col], axis=-1)
            upd = lax.dot_general(v_aug, kw, dn_0, preferred_element_type=F32)
            cn_ref[h] = decay * cn_ref[h] + upd
            st_ref[h, 0:1, 0:1] = f_end
            st_ref[h, 0:1, 1:2] = m_new

            hn_t = h_t * lax.rsqrt(jnp.mean(h_t * h_t, axis=0, keepdims=True) + EPS) * gnb_ref[h]
            y = jax.nn.sigmoid(o_ref[rows, cols].astype(F32)) * hn_t.T
            y_ref[rows, cols] = y.astype(y_ref.dtype)
        return 0

    lax.fori_loop(0, S // L, chunk, 0)


def mlstm(z3, conv_w, mlstm_g, g3, gt4, *, layer, L=CHUNK):
    B, S, _ = z3.shape
    blocks = [_WIDE_OFFSETS[n] // ML_WIDTH for n in ("ml_q", "ml_k", "ml_v", "ml_o")]
    seq_spec = lambda idx: pl.BlockSpec((None, S, ML_WIDTH), lambda b: (b, 0, idx))
    return pl.pallas_call(
        functools.partial(_mlstm_kernel, L=L),
        grid=(B,),
        in_specs=[
            seq_spec(blocks[0]), seq_spec(blocks[1]), seq_spec(blocks[2]), seq_spec(blocks[3]),
            _per_layer((CONV_WIDTH, 2 * ML_WIDTH), layer),
            _per_layer((ML_WIDTH, 1), layer),
            pl.BlockSpec((None, S, LANES), lambda b: (b, 0, 0)),
            pl.BlockSpec((None, S // L, N_GATE_LANES, L), lambda b: (b, 0, 0, 0)),
        ],
        out_specs=pl.BlockSpec((None, S, ML_WIDTH), lambda b: (b, 0, 0)),
        out_shape=jax.ShapeDtypeStruct((B, S, ML_WIDTH), BF16),
        scratch_shapes=[
            pltpu.VMEM((ML_HEADS, 2 * ML_HEAD_DIM, ML_HEAD_DIM), F32),
            pltpu.VMEM((ML_HEADS, 8, LANES), F32),
            pltpu.VMEM((ML_HALO, 2 * ML_WIDTH), F32),
            pltpu.VMEM((ML_HEADS, ML_HEAD_DIM, L), F32),
        ],
        compiler_params=_params(("parallel",), 40),
        name="mlstm",
    )(z3, z3, z3, z3, conv_w, mlstm_g, g3, gt4)


def _gelu_tanh(x):
    c0 = -2.0 * np.sqrt(2.0 / np.pi) * LOG2E
    c1 = c0 * 0.044715
    return x / (1.0 + jnp.exp2(x * (c0 + c1 * (x * x))))


def _gmlp_kernel(u_ref, v_ref, gn_ref, ws_ref, bst_ref, y_ref):
    rows = u_ref.shape[0]
    u = _gelu_tanh(u_ref[...].astype(F32))
    v = _gelu_tanh(v_ref[...].astype(F32))
    mu = jnp.mean(v, axis=-1, keepdims=True)
    vc = v - mu
    vn = (vc * lax.rsqrt(jnp.mean(vc * vc, axis=-1, keepdims=True) + EPS) * gn_ref[...]).astype(BF16)
    r = lax.broadcasted_iota(jnp.int32, (G_SPAN, G_SPAN), 0) // CHUNK
    c = lax.broadcasted_iota(jnp.int32, (G_SPAN, G_SPAN), 1) // CHUNK
    mask = r >= c
    for g in range(G_GROUPS):
        cols = slice(g * G_GROUP_DIM, (g + 1) * G_GROUP_DIM)
        w = jnp.where(mask, ws_ref[g], 0.0).astype(BF16)
        bias = bst_ref[:, g:g + 1]
        for s in range(rows // G_SPAN):
            rs = slice(s * G_SPAN, (s + 1) * G_SPAN)
            mixed = jnp.dot(w, vn[rs, cols], preferred_element_type=F32) + bias
            y_ref[rs, cols] = (u[rs, cols] * mixed).astype(y_ref.dtype)


def gmlp(z3, gmlp_g, ws, bs_t, *, layer, rows=1024):
    B, S, _ = z3.shape
    rows = min(rows, S)
    bu = _WIDE_OFFSETS["g_u"] // G_WIDTH
    bv = _WIDE_OFFSETS["g_v"] // G_WIDTH
    return pl.pallas_call(
        _gmlp_kernel,
        grid=(B, S // rows),
        in_specs=[
            pl.BlockSpec((None, rows, G_WIDTH), lambda b, i: (b, i, bu)),
            pl.BlockSpec((None, rows, G_WIDTH), lambda b, i: (b, i, bv)),
            _per_layer((1, G_WIDTH), layer),
            _per_layer((G_GROUPS, G_SPAN, G_SPAN), layer),
            _per_layer((G_SPAN, G_GROUPS), layer),
        ],
        out_specs=pl.BlockSpec((None, rows, G_WIDTH), lambda b, i: (b, i, 0)),
        out_shape=jax.ShapeDtypeStruct((B, S, G_WIDTH), BF16),
        compiler_params=_params(("parallel", "parallel"), 16),
        name="gmlp",
    )(z3, z3, gmlp_g, ws, bs_t)


def _merge_kernel(x_ref, gate_ref, ya_ref, yb_ref, yc_ref, wb_ref, wo_ref, gpost_ref, o_ref, *, sub):
    for r in range(x_ref.shape[0] // sub):
        rows = slice(r * sub, (r + 1) * sub)
        merged = None
        for n, y_ref in enumerate((ya_ref, yb_ref, yc_ref)):
            br = jnp.dot(y_ref[rows, :], wb_ref[n], preferred_element_type=F32)
            gt = jax.nn.sigmoid(gate_ref[rows, n * D_MODEL:(n + 1) * D_MODEL].astype(F32))
            merged = gt * br if merged is None else merged + gt * br
        y = jnp.dot(merged.astype(BF16), wo_ref[...], preferred_element_type=F32)
        o_ref[rows, :] = x_ref[rows, :] + _rms(y, gpost_ref[...])


def merge(x2, z2, y_fox, y_ml, y_g, w_branch, w_out, gains, *, layer, gain, tm=1024, sub=512):
    T, D = x2.shape
    tm = min(tm, T)
    row = lambda w: pl.BlockSpec((tm, w), lambda i: (i, 0))
    return pl.pallas_call(
        functools.partial(_merge_kernel, sub=min(sub, tm)),
        grid=(T // tm,),
        in_specs=[
            row(D),
            row(N_BRANCH * D),
            row(FOX_WIDTH), row(ML_WIDTH), row(G_WIDTH),
            _per_layer((N_BRANCH, FOX_WIDTH, D), layer),
            _per_layer((D, D), layer),
            _per_layer((1, D), gain),
        ],
        out_specs=row(D),
        out_shape=jax.ShapeDtypeStruct((T, D), F32),
        compiler_params=_params(("parallel",), 48),
        name="merge",
    )(x2, z2, y_fox, y_ml, y_g, w_branch, w_out, gains)


def _norm_matmul_kernel(x_ref, g_ref, w_ref, o_ref):
    h = _rms(x_ref[...], g_ref[...]).astype(BF16)
    o_ref[...] = jnp.dot(h, w_ref[...], preferred_element_type=F32).astype(o_ref.dtype)


def norm_matmul(x2, gains, w, *, layer, gain, tm=512):
    T, D = x2.shape
    N = w.shape[-1]
    tm = min(tm, T)
    return pl.pallas_call(
        _norm_matmul_kernel,
        grid=(T // tm,),
        in_specs=[
            pl.BlockSpec((tm, D), lambda i: (i, 0)),
            _per_layer((1, D), gain),
            _per_layer((D, N), layer),
        ],
        out_specs=pl.BlockSpec((tm, N), lambda i: (i, 0)),
        out_shape=jax.ShapeDtypeStruct((T, N), BF16),
        compiler_params=_params(("parallel",), 40),
        name="mem_kv",
    )(x2, gains, w)


def _xattn_kernel(x_ref, kv_ref, wq_ref, wo_ref, gpre_ref, gpost_ref, o_ref, *, sub):
    dn = (((1,), (1,)), ((), ()))
    for r in range(x_ref.shape[0] // sub):
        rows = slice(r * sub, (r + 1) * sub)
        x = x_ref[rows, :]
        h = _rms(x, gpre_ref[...]).astype(BF16)
        q = jnp.dot(h, wq_ref[...], preferred_element_type=F32).astype(BF16)
        outs = []
        for hd in range(X_HEADS):
            cols = slice(hd * X_HEAD_DIM, (hd + 1) * X_HEAD_DIM)
            k = kv_ref[:, cols]
            v = kv_ref[:, D_MODEL + hd * X_HEAD_DIM:D_MODEL + (hd + 1) * X_HEAD_DIM]
            s = lax.dot_general(q[:, cols], k, dn, preferred_element_type=F32) * (X_HEAD_DIM ** -0.5)
            s = s - jnp.max(s, axis=-1, keepdims=True)
            e = jnp.exp(s)
            ev = jnp.dot(e.astype(BF16), v, preferred_element_type=F32)
            outs.append((ev / jnp.sum(e, axis=-1, keepdims=True)).astype(BF16))
        o = jnp.concatenate(outs, axis=-1)
        y = jnp.dot(o, wo_ref[...], preferred_element_type=F32)
        o_ref[rows, :] = x + _rms(y, gpost_ref[...])


def xattn(x3, kv3, w_q, w_o, gains, *, layer, gain_pre, gain_post, tm=2048, sub=512):
    B, S, D = x3.shape
    M = kv3.shape[1]
    tm = min(tm, S)
    return pl.pallas_call(
        functools.partial(_xattn_kernel, sub=min(sub, tm)),
        grid=(B, S // tm),
        in_specs=[
            pl.BlockSpec((None, tm, D), lambda b, i: (b, i, 0)),
            pl.BlockSpec((None, M, 2 * D), lambda b, i: (b, 0, 0)),
            _per_layer((D, D), layer), _per_layer((D, D), layer),
            _per_layer((1, D), gain_pre), _per_layer((1, D), gain_post),
        ],
        out_specs=pl.BlockSpec((None, tm, D), lambda b, i: (b, i, 0)),
        out_shape=jax.ShapeDtypeStruct((B, S, D), F32),
        compiler_params=_params(("parallel", "arbitrary"), 56),
        name="xattn",
    )(x3, kv3, w_q, w_o, gains, gains)


def _ffn_kernel(x_ref, w1_ref, w2_ref, gpre_ref, gpost_ref, o_ref, *, ff_chunk, sub):
    for r in range(x_ref.shape[0] // sub):
        rows = slice(r * sub, (r + 1) * sub)
        x = x_ref[rows, :]
        h = _rms(x, gpre_ref[...]).astype(BF16)
        acc = None
        for c in range(w1_ref.shape[1] // ff_chunk):
            cs = slice(c * ff_chunk, (c + 1) * ff_chunk)
            a = jnp.dot(h, w1_ref[:, cs], preferred_element_type=F32)
            a = jnp.square(jnp.maximum(a, 0.0)).astype(BF16)
            part = jnp.dot(a, w2_ref[cs, :], preferred_element_type=F32)
            acc = part if acc is None else acc + part
        o_ref[rows, :] = x + _rms(acc, gpost_ref[...])


def ffn(x2, w1, w2, gains, *, layer, gain_pre, gain_post, tm=1024, sub=512, ff_chunk=1024):
    T, D = x2.shape
    FF = w1.shape[-1]
    tm = min(tm, T)
    return pl.pallas_call(
        functools.partial(_ffn_kernel, ff_chunk=ff_chunk, sub=min(sub, tm)),
        grid=(T // tm,),
        in_specs=[
            pl.BlockSpec((tm, D), lambda i: (i, 0)),
            _per_layer((D, FF), layer),
            _per_layer((FF, D), layer),
            _per_layer((1, D), gain_pre),
            _per_layer((1, D), gain_post),
        ],
        out_specs=pl.BlockSpec((tm, D), lambda i: (i, 0)),
        out_shape=jax.ShapeDtypeStruct((T, D), F32),
        compiler_params=_params(("parallel",), 56),
        name="ffn",
    )(x2, w1, w2, gains, gains)


def _regroup_kernel(wt_ref, wide_ref, narrow_ref):
    for name in _WIDE_ORDER:
        src, size = _IN_OFFSETS[name]
        dst = _WIDE_OFFSETS[name]
        wide_ref[dst:dst + size, :] = wt_ref[src:src + size, :].astype(BF16)
    narrow_ref[...] = jnp.zeros_like(narrow_ref)
    dst = 0
    for name in _NARROW_ORDER:
        src, size = _IN_OFFSETS[name]
        narrow_ref[dst:dst + size, :] = wt_ref[src:src + size, :].astype(BF16)
        dst += size


def _regroup_in_proj(w_in, b_in, *, cols_per_step=256):
    depth, D, d_in = w_in.shape
    w_in_t = jnp.swapaxes(w_in, 1, 2)
    w_wide, w_narrow = pl.pallas_call(
        _regroup_kernel,
        grid=(depth, D // cols_per_step),
        in_specs=[pl.BlockSpec((None, d_in, cols_per_step), lambda l, i: (l, 0, i))],
        out_specs=[pl.BlockSpec((None, N_WIDE, cols_per_step), lambda l, i: (l, 0, i)),
                   pl.BlockSpec((None, LANES, cols_per_step), lambda l, i: (l, 0, i))],
        out_shape=[jax.ShapeDtypeStruct((depth, N_WIDE, D), BF16),
                   jax.ShapeDtypeStruct((depth, LANES, D), BF16)],
        compiler_params=_params(("parallel", "parallel"), 48),
        name="regroup",
    )(w_in_t)

    def cols(names):
        return [slice(_IN_OFFSETS[n][0], _IN_OFFSETS[n][0] + _IN_OFFSETS[n][1]) for n in names]
    b_wide = jnp.concatenate([b_in[..., s] for s in cols(_WIDE_ORDER)], axis=-1)[:, None, :]
    pad = LANES - N_GATE_LANES
    b_narrow = jnp.pad(jnp.concatenate([b_in[..., s] for s in cols(_NARROW_ORDER)], axis=-1),
                       ((0, 0), (0, pad)))[:, None, :]
    return w_wide, b_wide, w_narrow, b_narrow


def _prepare_params(norms, w_in, b_in, conv_w, mlstm_norm, gmlp_norm, gmlp_ws, gmlp_bs,
                    w_branch, w_out, w_xq, w_xkv, w_xo, w_ff1, w_ff2):
    depth = norms.shape[0]
    w_wide, b_wide, w_narrow, b_narrow = _regroup_in_proj(w_in, b_in)
    return dict(
        gains=norms.reshape(depth * N_NORMS, 1, D_MODEL),
        w_wide=w_wide, b_wide=b_wide, w_narrow=w_narrow, b_narrow=b_narrow,
        conv_w=conv_w, mlstm_g=mlstm_norm[:, :, None], gmlp_g=gmlp_norm[:, None, :],
        gmlp_ws=gmlp_ws, gmlp_bs_t=gmlp_bs.transpose(0, 2, 1),
        w_branch=w_branch.astype(BF16), w_out=w_out.astype(BF16),
        w_xq=w_xq.astype(BF16), w_xkv=w_xkv.astype(BF16), w_xo=w_xo.astype(BF16),
        w_ff1=w_ff1.astype(BF16), w_ff2=w_ff2.astype(BF16),
    )


def _layer(x3, mem2, p, layer, *, fox_blk, ml_chunk):
    B, S, D = x3.shape
    T = B * S
    M = mem2.shape[0] // B
    gain = lambda idx: layer * N_NORMS + idx
    x2 = x3.reshape(T, D)

    z2, zn2 = in_proj(x2, p["gains"], p["w_wide"], p["b_wide"], p["w_narrow"], p["b_narrow"],
                      layer=layer, gain=gain(NORM_MIX_PRE))
    z3 = z2.reshape(B, S, N_WIDE)
    g3, gt4 = gates(zn2.reshape(B, S, LANES), L=ml_chunk)

    y_fox = fox_attention(z3, g3, blk=fox_blk, qblk=min(2 * fox_blk, S))
    y_ml = mlstm(z3, p["conv_w"], p["mlstm_g"], g3, gt4, layer=layer, L=ml_chunk)
    y_g = gmlp(z3, p["gmlp_g"], p["gmlp_ws"], p["gmlp_bs_t"], layer=layer)
    x2 = merge(x2, z2, y_fox.reshape(T, FOX_WIDTH), y_ml.reshape(T, ML_WIDTH), y_g.reshape(T, G_WIDTH),
               p["w_branch"], p["w_out"], p["gains"], layer=layer, gain=gain(NORM_MIX_POST))

    kv = norm_matmul(mem2, p["gains"], p["w_xkv"], layer=layer, gain=gain(NORM_MEM))
    x3 = xattn(x2.reshape(B, S, D), kv.reshape(B, M, 2 * D), p["w_xq"], p["w_xo"], p["gains"],
               layer=layer, gain_pre=gain(NORM_X_PRE), gain_post=gain(NORM_X_POST))

    x2 = ffn(x3.reshape(T, D), p["w_ff1"], p["w_ff2"], p["gains"],
             layer=layer, gain_pre=gain(NORM_FF_PRE), gain_post=gain(NORM_FF_POST))
    return x2.reshape(B, S, D)


def kernel(x, mem, norms, w_in, b_in, conv_w, mlstm_norm, gmlp_norm, gmlp_ws, gmlp_bs, w_branch, w_out, w_xq, w_xkv, w_xo, w_ff1, w_ff2):
    B, M, D = mem.shape
    S = x.shape[1]
    mem2 = mem.reshape(B * M, D)
    p = _prepare_params(norms, w_in, b_in, conv_w, mlstm_norm, gmlp_norm, gmlp_ws, gmlp_bs,
                        w_branch, w_out, w_xq, w_xkv, w_xo, w_ff1, w_ff2)
    for layer in range(norms.shape[0]):
        x = _layer(x, mem2, p, layer, fox_blk=min(256, S), ml_chunk=min(ML_CHUNK, S))
    return x
```

```python
import functools

import jax
import jax.numpy as jnp
import numpy as np
from jax import lax
from jax.experimental import pallas as pl
from jax.experimental.pallas import tpu as pltpu

F32 = jnp.float32
BF16 = jnp.bfloat16

EPS = 1e-6
D_MODEL = 1024
CHUNK = 64
FOX_HEADS = 8
FOX_HEAD_DIM = 64
FOX_WIDTH = FOX_HEADS * FOX_HEAD_DIM
ML_HEADS = 4
ML_HEAD_DIM = 128
ML_WIDTH = ML_HEADS * ML_HEAD_DIM
ML_CHUNK = 256
ML_HALO = 8
CONV_WIDTH = 4
G_GROUPS = 4
G_GROUP_DIM = 128
G_WIDTH = G_GROUPS * G_GROUP_DIM
G_SPAN = 128
N_BRANCH = 3
X_HEADS = 4
X_HEAD_DIM = D_MODEL // X_HEADS
D_FF = 4 * D_MODEL
NORM_MIX_PRE, NORM_MIX_POST, NORM_X_PRE, NORM_X_POST, NORM_MEM, NORM_FF_PRE, NORM_FF_POST = range(7)
N_NORMS = 7

LOG2E = 1.4426950408889634
LANES = 128
MIB = 1024 * 1024

_IN_SPLITS = (
    ("fox_q", FOX_WIDTH), ("fox_k", FOX_WIDTH), ("fox_v", FOX_WIDTH), ("fox_f", FOX_HEADS),
    ("ml_q", ML_WIDTH), ("ml_k", ML_WIDTH), ("ml_v", ML_WIDTH),
    ("ml_i", ML_HEADS), ("ml_f", ML_HEADS), ("ml_o", ML_WIDTH),
    ("g_u", G_WIDTH), ("g_v", G_WIDTH),
    ("gate", N_BRANCH * D_MODEL),
)
_IN_OFFSETS = {}
_off = 0
for _name, _size in _IN_SPLITS:
    _IN_OFFSETS[_name] = (_off, _size)
    _off += _size

_WIDE_ORDER = ("gate", "fox_q", "fox_k", "fox_v", "ml_q", "ml_k", "ml_v", "ml_o", "g_u", "g_v")
_WIDE_OFFSETS = {}
_off = 0
for _name in _WIDE_ORDER:
    _WIDE_OFFSETS[_name] = _off
    _off += _IN_OFFSETS[_name][1]
N_WIDE = _off
_NARROW_ORDER = ("fox_f", "ml_i", "ml_f")
LANE_FOX_F = 0
LANE_ML_I = FOX_HEADS
LANE_ML_F = FOX_HEADS + ML_HEADS
N_GATE_LANES = FOX_HEADS + 2 * ML_HEADS


def _rms(x, g):
    return x * lax.rsqrt(jnp.mean(x * x, axis=-1, keepdims=True) + EPS) * g


def _log_sigmoid(x):
    return jnp.minimum(x, 0.0) - jnp.log1p(jnp.exp(-jnp.abs(x)))


def _params(semantics, vmem_mib):
    return pltpu.CompilerParams(dimension_semantics=semantics, vmem_limit_bytes=vmem_mib * MIB)


def _per_layer(block, layer, tail=None):
    block = tuple(block)
    if tail is None:
        return pl.BlockSpec((None,) + block, lambda *ids: (layer,) + (0,) * len(block),
                            pipeline_mode=pl.Buffered(1))
    return pl.BlockSpec((None,) + block, lambda *ids: (layer,) + tuple(tail(*ids)))


def _in_proj_kernel(x_ref, g_ref, wt_ref, b_ref, wnt_ref, bn_ref, z_ref, zn_ref, h_ref, *, sub):
    j = pl.program_id(1)
    nt = (((1,), (1,)), ((), ()))

    @pl.when(j == 0)
    def _():
        for r in range(x_ref.shape[0] // sub):
            rows = slice(r * sub, (r + 1) * sub)
            h = _rms(x_ref[rows, :], g_ref[...]).astype(BF16)
            h_ref[rows, :] = h
            zn_ref[rows, :] = lax.dot_general(h, wnt_ref[...], nt, preferred_element_type=F32) + bn_ref[...]
            z = lax.dot_general(h, wt_ref[...], nt, preferred_element_type=F32) + b_ref[...]
            z_ref[rows, :] = z.astype(z_ref.dtype)

    @pl.when(j != 0)
    def _():
        z = lax.dot_general(h_ref[...], wt_ref[...], nt, preferred_element_type=F32) + b_ref[...]
        z_ref[...] = z.astype(z_ref.dtype)


def in_proj(x2, gains, w_wide, b_wide, w_narrow, b_narrow, *, layer, gain, tm=2048, tn=1536, sub=512):
    T, D = x2.shape
    N = w_wide.shape[1]
    tm = min(tm, T)
    return pl.pallas_call(
        functools.partial(_in_proj_kernel, sub=min(sub, tm)),
        grid=(T // tm, N // tn),
        in_specs=[
            pl.BlockSpec((tm, D), lambda i, j: (i, 0)),
            _per_layer((1, D), gain),
            _per_layer((tn, D), layer, lambda i, j: (j, 0)),
            _per_layer((1, tn), layer, lambda i, j: (0, j)),
            _per_layer((LANES, D), layer),
            _per_layer((1, LANES), layer),
        ],
        out_specs=[
            pl.BlockSpec((tm, tn), lambda i, j: (i, j)),
            pl.BlockSpec((tm, LANES), lambda i, j: (i, 0)),
        ],
        out_shape=[
            jax.ShapeDtypeStruct((T, N), BF16),
            jax.ShapeDtypeStruct((T, LANES), F32),
        ],
        scratch_shapes=[pltpu.VMEM((tm, D), BF16)],
        compiler_params=_params(("parallel", "arbitrary"), 56),
        name="in_proj",
    )(x2, gains, w_wide, b_wide, w_narrow, b_narrow)


def _gates_kernel(zn_ref, g_ref, gt_ref, *, rows):
    S = zn_ref.shape[0]
    L = gt_ref.shape[-1]
    lane = lax.broadcasted_iota(jnp.int32, (rows, LANES), 1)
    keep_raw = jnp.logical_and(lane >= LANE_ML_I, lane < LANE_ML_F)
    r = lax.broadcasted_iota(jnp.int32, (rows, rows), 0)
    c = lax.broadcasted_iota(jnp.int32, (rows, rows), 1)
    tril = jnp.where(r >= c, 1.0, 0.0).astype(F32)

    offset = jnp.zeros((1, LANES), F32)
    for i in range(S // rows):
        sl = slice(i * rows, (i + 1) * rows)
        z = zn_ref[sl, :]
        ls = _log_sigmoid(z)
        cs = jnp.dot(tril, ls, precision=lax.Precision.HIGHEST, preferred_element_type=F32) + offset
        out = jnp.where(keep_raw, z, cs)
        g_ref[sl, :] = out
        out_t = out.T
        for k in range(rows // L):
            gt_ref[i * (rows // L) + k] = out_t[:N_GATE_LANES, k * L:(k + 1) * L]
        offset = cs[rows - 1:rows, :]


def gates(zn3, *, L, rows=256):
    B, S, _ = zn3.shape
    rows = min(rows, S)
    assert rows % L == 0
    return pl.pallas_call(
        functools.partial(_gates_kernel, rows=rows),
        grid=(B,),
        in_specs=[pl.BlockSpec((None, S, LANES), lambda b: (b, 0, 0))],
        out_specs=[pl.BlockSpec((None, S, LANES), lambda b: (b, 0, 0)),
                   pl.BlockSpec((None, S // L, N_GATE_LANES, L), lambda b: (b, 0, 0, 0))],
        out_shape=[jax.ShapeDtypeStruct((B, S, LANES), F32),
                   jax.ShapeDtypeStruct((B, S // L, N_GATE_LANES, L), F32)],
        compiler_params=_params(("parallel",), 16),
        name="gates",
    )(zn3)


FOX_AUG = LANES
FOX_SUB = 8
FOX_PAIR = 2 * FOX_HEAD_DIM
FOX_PAIRS = FOX_HEADS // 2
FOX_VT_ROWS = FOX_HEAD_DIM + 16


def _fox_place_matrices():
    pk = np.zeros((3 * LANES, FOX_PAIRS * FOX_AUG), np.float32)
    pq = np.zeros((3 * LANES, FOX_PAIRS * FOX_AUG), np.float32)
    ck = np.zeros((1, FOX_PAIRS * FOX_AUG), np.float32)
    cq = np.zeros((1, FOX_PAIRS * FOX_AUG), np.float32)
    for h in range(FOX_HEADS):
        base = (h // 2) * FOX_AUG + (h % 2) * FOX_SUB
        for piece in range(3):
            pk[piece * LANES + LANE_FOX_F + h, base + piece] = -1.0
            pq[piece * LANES + LANE_FOX_F + h, base + 3 + piece] = 1.0
            ck[0, base + 3 + piece] = 1.0
            cq[0, base + piece] = 1.0
    return (jnp.asarray(pk, BF16), jnp.asarray(pq, BF16), jnp.asarray(ck), jnp.asarray(cq))


def _fox_bias_lanes(f, place_ref, ones_ref):
    f = f * LOG2E
    hi = f.astype(BF16)
    r1 = f - hi.astype(F32)
    mid = r1.astype(BF16)
    lo = (r1 - mid.astype(F32)).astype(BF16)
    x = jnp.concatenate([hi, mid, lo], axis=-1)
    return (jnp.dot(x, place_ref[...], preferred_element_type=F32) + ones_ref[...]).astype(BF16)


def _fox_kernel(q_ref, k_ref, v_ref, g_ref, pk_ref, pq_ref, ck_ref, cq_ref, o_ref,
                kaug_ref, vt_ref, qaug_ref, m_ref, acc_ref, al_ref, st_ref, p_ref, *, blk, qblk):
    S = k_ref.shape[0]
    ratio = qblk // blk
    qi = pl.program_id(1)

    @pl.when(qi == 0)
    def _():
        ones = jnp.ones((FOX_VT_ROWS - FOX_HEAD_DIM, blk), BF16)
        for c in range(S // blk):
            rows = slice(c * blk, (c + 1) * blk)
            kaug_ref[rows, :] = _fox_bias_lanes(g_ref[rows, :], pk_ref, ck_ref)
            for j in range(FOX_PAIRS):
                vt = v_ref[rows, j * FOX_PAIR:(j + 1) * FOX_PAIR].T
                for hh in range(2):
                    vt_ref[c, 2 * j + hh, :FOX_HEAD_DIM, :] = vt[hh * FOX_HEAD_DIM:(hh + 1) * FOX_HEAD_DIM, :]
                    vt_ref[c, 2 * j + hh, FOX_HEAD_DIM:, :] = ones

    q_rows = pl.ds(pl.multiple_of(qi * qblk, qblk), qblk)
    qaug = _fox_bias_lanes(g_ref[q_rows, :], pq_ref, cq_ref)
    q_half = lax.broadcasted_iota(jnp.int32, (qblk, FOX_PAIR), 1) // FOX_HEAD_DIM
    aug_half = lax.broadcasted_iota(jnp.int32, (qblk, FOX_AUG), 1) // FOX_SUB
    zero = jnp.zeros((), BF16)
    scale = FOX_HEAD_DIM ** -0.5 * LOG2E
    for j in range(FOX_PAIRS):
        qs = (q_ref[:, j * FOX_PAIR:(j + 1) * FOX_PAIR].astype(F32) * scale).astype(BF16)
        qa = qaug[:, j * FOX_AUG:(j + 1) * FOX_AUG]
        both = [jnp.concatenate([jnp.where(q_half == hh, qs, zero), jnp.where(aug_half == hh, qa, zero)], axis=-1).T
                for hh in range(2)]
        qaug_ref[j] = jnp.concatenate(both, axis=-1)
    m_ref[...] = jnp.full_like(m_ref, -jnp.inf)
    acc_ref[...] = jnp.zeros_like(acc_ref)

    def score_pair(kb, slot, j, q0=0):
        ks = pl.ds(pl.multiple_of(kb * blk, blk), blk)
        kk = jnp.concatenate([k_ref[ks, j * FOX_PAIR:(j + 1) * FOX_PAIR],
                              kaug_ref[ks, j * FOX_AUG:(j + 1) * FOX_AUG]], axis=-1)
        if q0 == 0:
            st_ref[slot, j] = jnp.dot(kk, qaug_ref[j], preferred_element_type=F32)
        else:
            for hh in range(2):
                window = slice(hh * qblk + q0, (hh + 1) * qblk)
                st_ref[slot, j, :, window] = jnp.dot(kk, qaug_ref[j, :, window], preferred_element_type=F32)

    def softmax_head(slot, h, diag, q0=0):
        st = st_ref[slot, h // 2, :, (h % 2) * qblk + q0:(h % 2 + 1) * qblk]
        if diag:
            key_pos = lax.broadcasted_iota(jnp.int32, (blk, qblk - q0), 0)
            qry_pos = lax.broadcasted_iota(jnp.int32, (blk, qblk - q0), 1)
            st = jnp.where(key_pos <= qry_pos, st, -jnp.inf)
        m_row = m_ref[h:h + 1, :]
        m_old = m_row[:, q0:]
        m_new = jnp.maximum(m_old, jnp.max(st, axis=0, keepdims=True))
        alpha = jnp.exp2(m_old - m_new)
        if q0 == 0:
            al_ref[h:h + 1, :] = alpha
            m_ref[h:h + 1, :] = m_new
        else:
            al_ref[h:h + 1, :] = jnp.concatenate([jnp.ones((1, q0), F32), alpha], axis=-1)
            m_ref[h:h + 1, :] = jnp.concatenate([m_row[:, :q0], m_new], axis=-1)
        p_ref[h, :, q0:] = jnp.exp2(st - m_new).astype(BF16)

    def value_head(kb, h, q0=0):
        pv = jnp.dot(vt_ref[kb, h], p_ref[h, :, q0:], preferred_element_type=F32)
        alpha = al_ref[h:h + 1, :]
        acc_ref[h, :, q0:] = alpha[:, q0:] * acc_ref[h, :, q0:] + pv

    def advance(kb, slot, diag, q0=0, next_q0=None):
        for j in range(FOX_PAIRS):
            if next_q0 is not None:
                score_pair(kb + 1, 1 - slot, j, next_q0)
            softmax_head(slot, 2 * j, diag, q0)
            softmax_head(slot, 2 * j + 1, diag, q0)
            if j > 0:
                value_head(kb, 2 * j - 2, q0)
                value_head(kb, 2 * j - 1, q0)
        value_head(kb, FOX_HEADS - 2, q0)
        value_head(kb, FOX_HEADS - 1, q0)

    for j in range(FOX_PAIRS):
        score_pair(0, 0, j)

    def two_blocks(i, carry):
        advance(2 * i, 0, False, next_q0=0)
        advance(2 * i + 1, 1, False, next_q0=0)
        return carry

    first_diag = ratio * qi
    lax.fori_loop(0, first_diag // 2, two_blocks, 0)
    for d in range(ratio):
        advance(first_diag + d, d % 2, True, q0=d * blk, next_q0=(d + 1) * blk if d + 1 < ratio else None)

    outs = []
    for j in range(FOX_HEADS // 2):
        tops = []
        for h in (2 * j, 2 * j + 1):
            a = acc_ref[h]
            tops.append(a[:FOX_HEAD_DIM, :] / a[FOX_HEAD_DIM:FOX_HEAD_DIM + 1, :])
        outs.append(jnp.concatenate(tops, axis=0).T)
    o_ref[...] = jnp.concatenate(outs, axis=-1).astype(o_ref.dtype)


def fox_attention(z3, g3, *, blk=256, qblk=512):
    B, S, _ = z3.shape
    assert qblk % (2 * blk) == 0 and S % qblk == 0
    wq = _WIDE_OFFSETS["fox_q"] // FOX_WIDTH
    wk = _WIDE_OFFSETS["fox_k"] // FOX_WIDTH
    wv = _WIDE_OFFSETS["fox_v"] // FOX_WIDTH
    pk, pq, ck, cq = _fox_place_matrices()
    const = lambda a: pl.BlockSpec(a.shape, lambda b, i: (0,) * a.ndim)
    return pl.pallas_call(
        functools.partial(_fox_kernel, blk=blk, qblk=qblk),
        grid=(B, S // qblk),
        in_specs=[
            pl.BlockSpec((None, qblk, FOX_WIDTH), lambda b, i: (b, i, wq)),
            pl.BlockSpec((None, S, FOX_WIDTH), lambda b, i: (b, 0, wk)),
            pl.BlockSpec((None, S, FOX_WIDTH), lambda b, i: (b, 0, wv)),
            pl.BlockSpec((None, S, LANES), lambda b, i: (b, 0, 0)),
            const(pk), const(pq), const(ck), const(cq),
        ],
        out_specs=pl.BlockSpec((None, qblk, FOX_WIDTH), lambda b, i: (b, i, 0)),
        out_shape=jax.ShapeDtypeStruct((B, S, FOX_WIDTH), BF16),
        scratch_shapes=[
            pltpu.VMEM((S, FOX_PAIRS * FOX_AUG), BF16),
            pltpu.VMEM((S // blk, FOX_HEADS, FOX_VT_ROWS, blk), BF16),
            pltpu.VMEM((FOX_PAIRS, FOX_PAIR + FOX_AUG, 2 * qblk), BF16),
            pltpu.VMEM((FOX_HEADS, qblk), F32),
            pltpu.VMEM((FOX_HEADS, FOX_VT_ROWS, qblk), F32),
            pltpu.VMEM((FOX_HEADS, qblk), F32),
            pltpu.VMEM((2, FOX_PAIRS, blk, 2 * qblk), F32),
            pltpu.VMEM((FOX_HEADS, blk, qblk), BF16),
        ],
        compiler_params=_params(("parallel", "arbitrary"), 56),
        name="fox",
    )(z3, z3, z3, g3, pk, pq, ck, cq)


def _mlstm_kernel(q_ref, k_ref, v_ref, o_ref, cw_ref, gn_ref, g_ref, gt_ref, y_ref,
                  cn_ref, st_ref, halo_ref, gnb_ref, *, L):
    S = q_ref.shape[0]
    HALO = halo_ref.shape[0]
    src = lax.broadcasted_iota(jnp.int32, (L, L), 0)
    qry = lax.broadcasted_iota(jnp.int32, (L, L), 1)
    triu = src <= qry
    lane = lax.broadcasted_iota(jnp.int32, (L, LANES), 1)
    ones_col = jnp.where(lane == 0, 1.0, 0.0).astype(BF16)
    k_scale = ML_HEAD_DIM ** -0.5
    dn_t = (((1,), (1,)), ((), ()))
    dn_0 = (((0,), (0,)), ((), ()))

    cn_ref[...] = jnp.zeros_like(cn_ref)
    st_ref[...] = jnp.zeros_like(st_ref)
    halo_ref[...] = jnp.zeros_like(halo_ref)
    for h in range(ML_HEADS):
        gnb_ref[h] = jnp.broadcast_to(gn_ref[h * ML_HEAD_DIM:(h + 1) * ML_HEAD_DIM, :], (ML_HEAD_DIM, L))

    def conv_silu(x_chunk, halo, w):
        xx = jnp.concatenate([halo, x_chunk], axis=0)
        y = w[CONV_WIDTH - 1:CONV_WIDTH, :] * x_chunk
        for j in range(CONV_WIDTH - 1):
            sh = CONV_WIDTH - 1 - j
            y = y + w[j:j + 1, :] * xx[HALO - sh:HALO - sh + L, :]
        return y * jax.nn.sigmoid(y)

    def chunk(ci, _):
        rows = pl.ds(pl.multiple_of(ci * L, L), L)
        xq = q_ref[rows, :].astype(F32)
        xk = k_ref[rows, :].astype(F32)
        qa = conv_silu(xq, halo_ref[:, :ML_WIDTH], cw_ref[:, :ML_WIDTH])
        ka = conv_silu(xk, halo_ref[:, ML_WIDTH:], cw_ref[:, ML_WIDTH:]) * k_scale
        halo_ref[:, :ML_WIDTH] = xq[L - HALO:, :]
        halo_ref[:, ML_WIDTH:] = xk[L - HALO:, :]
        gcol = g_ref[rows, :]
        grow = gt_ref[ci]

        for h in range(ML_HEADS):
            cols = slice(h * ML_HEAD_DIM, (h + 1) * ML_HEAD_DIM)
            qb = qa[:, cols].astype(BF16)
            kf = ka[:, cols]
            vb = v_ref[rows, cols]
            c_col = gcol[:, LANE_ML_I + h:LANE_ML_I + h + 1] - gcol[:, LANE_ML_F + h:LANE_ML_F + h + 1]
            f_r = grow[LANE_ML_F + h:LANE_ML_F + h + 1, :]
            f_prev = st_ref[h, 0:1, 0:1]
            m_prev = st_ref[h, 0:1, 1:2]

            d_t = jnp.where(triu, c_col + f_r, -jnp.inf)
            inter = f_r - f_prev + m_prev
            m = jnp.maximum(inter, jnp.max(d_t, axis=0, keepdims=True))
            w_inter = jnp.exp(inter - m)
            p_t = lax.dot_general(kf.astype(BF16), qb, dn_t, preferred_element_type=F32) * jnp.exp(d_t - m)
            io_t = lax.dot_general(cn_ref[h].astype(BF16), qb, dn_t, preferred_element_type=F32)
            pv_t = lax.dot_general(vb, p_t.astype(BF16), dn_0, preferred_element_type=F32)
            num = w_inter * io_t[:ML_HEAD_DIM, :] + pv_t
            den = w_inter * io_t[ML_HEAD_DIM:ML_HEAD_DIM + 1, :] + jnp.sum(p_t, axis=0, keepdims=True)
            h_t = num / jnp.maximum(jnp.abs(den), jnp.exp(-m))

            m_new = m[:, L - 1:L]
            f_end = f_r[:, L - 1:L]
            decay = jnp.exp(f_end - f_prev + m_prev - m_new)
            w_s = jnp.exp(jnp.broadcast_to(c_col, (L, ML_HEAD_DIM)) + (f_end - m_new))
            kw = (w_s * kf).astype(BF16)
            v_aug = jnp.concatenate([vb, ones_col], axis=-1)
            upd = lax.dot_general(v_aug, kw, dn_0, preferred_element_type=F32)
            cn_ref[h] = decay * cn_ref[h] + upd
            st_ref[h, 0:1, 0:1] = f_end
            st_ref[h, 0:1, 1:2] = m_new

            hn_t = h_t * lax.rsqrt(jnp.mean(h_t * h_t, axis=0, keepdims=True) + EPS) * gnb_ref[h]
            y = jax.nn.sigmoid(o_ref[rows, cols].astype(F32)) * hn_t.T
            y_ref[rows, cols] = y.astype(y_ref.dtype)
        return 0

    lax.fori_loop(0, S // L, chunk, 0)


def mlstm(z3, conv_w, mlstm_g, g3, gt4, *, layer, L=CHUNK):
    B, S, _ = z3.shape
    blocks = [_WIDE_OFFSETS[n] // ML_WIDTH for n in ("ml_q", "ml_k", "ml_v", "ml_o")]
    seq_spec = lambda idx: pl.BlockSpec((None, S, ML_WIDTH), lambda b: (b, 0, idx))
    return pl.pallas_call(
        functools.partial(_mlstm_kernel, L=L),
        grid=(B,),
        in_specs=[
            seq_spec(blocks[0]), seq_spec(blocks[1]), seq_spec(blocks[2]), seq_spec(blocks[3]),
            _per_layer((CONV_WIDTH, 2 * ML_WIDTH), layer),
            _per_layer((ML_WIDTH, 1), layer),
            pl.BlockSpec((None, S, LANES), lambda b: (b, 0, 0)),
            pl.BlockSpec((None, S // L, N_GATE_LANES, L), lambda b: (b, 0, 0, 0)),
        ],
        out_specs=pl.BlockSpec((None, S, ML_WIDTH), lambda b: (b, 0, 0)),
        out_shape=jax.ShapeDtypeStruct((B, S, ML_WIDTH), BF16),
        scratch_shapes=[
            pltpu.VMEM((ML_HEADS, 2 * ML_HEAD_DIM, ML_HEAD_DIM), F32),
            pltpu.VMEM((ML_HEADS, 8, LANES), F32),
            pltpu.VMEM((ML_HALO, 2 * ML_WIDTH), F32),
            pltpu.VMEM((ML_HEADS, ML_HEAD_DIM, L), F32),
        ],
        compiler_params=_params(("parallel",), 40),
        name="mlstm",
    )(z3, z3, z3, z3, conv_w, mlstm_g, g3, gt4)


def _gelu_tanh(x):
    c0 = -2.0 * np.sqrt(2.0 / np.pi) * LOG2E
    c1 = c0 * 0.044715
    return x / (1.0 + jnp.exp2(x * (c0 + c1 * (x * x))))


def _gmlp_kernel(u_ref, v_ref, gn_ref, ws_ref, bst_ref, y_ref):
    rows = u_ref.shape[0]
    u = _gelu_tanh(u_ref[...].astype(F32))
    v = _gelu_tanh(v_ref[...].astype(F32))
    mu = jnp.mean(v, axis=-1, keepdims=True)
    vc = v - mu
    vn = (vc * lax.rsqrt(jnp.mean(vc * vc, axis=-1, keepdims=True) + EPS) * gn_ref[...]).astype(BF16)
    r = lax.broadcasted_iota(jnp.int32, (G_SPAN, G_SPAN), 0) // CHUNK
    c = lax.broadcasted_iota(jnp.int32, (G_SPAN, G_SPAN), 1) // CHUNK
    mask = r >= c
    for g in range(G_GROUPS):
        cols = slice(g * G_GROUP_DIM, (g + 1) * G_GROUP_DIM)
        w = jnp.where(mask, ws_ref[g], 0.0).astype(BF16)
        bias = bst_ref[:, g:g + 1]
        for s in range(rows // G_SPAN):
            rs = slice(s * G_SPAN, (s + 1) * G_SPAN)
            mixed = jnp.dot(w, vn[rs, cols], preferred_element_type=F32) + bias
            y_ref[rs, cols] = (u[rs, cols] * mixed).astype(y_ref.dtype)


def gmlp(z3, gmlp_g, ws, bs_t, *, layer, rows=1024):
    B, S, _ = z3.shape
    rows = min(rows, S)
    bu = _WIDE_OFFSETS["g_u"] // G_WIDTH
    bv = _WIDE_OFFSETS["g_v"] // G_WIDTH
    return pl.pallas_call(
        _gmlp_kernel,
        grid=(B, S // rows),
        in_specs=[
            pl.BlockSpec((None, rows, G_WIDTH), lambda b, i: (b, i, bu)),
            pl.BlockSpec((None, rows, G_WIDTH), lambda b, i: (b, i, bv)),
            _per_layer((1, G_WIDTH), layer),
            _per_layer((G_GROUPS, G_SPAN, G_SPAN), layer),
            _per_layer((G_SPAN, G_GROUPS), layer),
        ],
        out_specs=pl.BlockSpec((None, rows, G_WIDTH), lambda b, i: (b, i, 0)),
        out_shape=jax.ShapeDtypeStruct((B, S, G_WIDTH), BF16),
        compiler_params=_params(("parallel", "parallel"), 16),
        name="gmlp",
    )(z3, z3, gmlp_g, ws, bs_t)


def _merge_kernel(x_ref, gate_ref, ya_ref, yb_ref, yc_ref, wb_ref, wo_ref, gpost_ref, o_ref, *, sub):
    for r in range(x_ref.shape[0] // sub):
        rows = slice(r * sub, (r + 1) * sub)
        merged = None
        for n, y_ref in enumerate((ya_ref, yb_ref, yc_ref)):
            br = jnp.dot(y_ref[rows, :], wb_ref[n], preferred_element_type=F32)
            gt = jax.nn.sigmoid(gate_ref[rows, n * D_MODEL:(n + 1) * D_MODEL].astype(F32))
            merged = gt * br if merged is None else merged + gt * br
        y = jnp.dot(merged.astype(BF16), wo_ref[...], preferred_element_type=F32)
        o_ref[rows, :] = x_ref[rows, :] + _rms(y, gpost_ref[...])


def merge(x2, z2, y_fox, y_ml, y_g, w_branch, w_out, gains, *, layer, gain, tm=1024, sub=512):
    T, D = x2.shape
    tm = min(tm, T)
    row = lambda w: pl.BlockSpec((tm, w), lambda i: (i, 0))
    return pl.pallas_call(
        functools.partial(_merge_kernel, sub=min(sub, tm)),
        grid=(T // tm,),
        in_specs=[
            row(D),
            row(N_BRANCH * D),
            row(FOX_WIDTH), row(ML_WIDTH), row(G_WIDTH),
            _per_layer((N_BRANCH, FOX_WIDTH, D), layer),
            _per_layer((D, D), layer),
            _per_layer((1, D), gain),
        ],
        out_specs=row(D),
        out_shape=jax.ShapeDtypeStruct((T, D), F32),
        compiler_params=_params(("parallel",), 48),
        name="merge",
    )(x2, z2, y_fox, y_ml, y_g, w_branch, w_out, gains)


def _norm_matmul_kernel(x_ref, g_ref, w_ref, o_ref):
    h = _rms(x_ref[...], g_ref[...]).astype(BF16)
    o_ref[...] = jnp.dot(h, w_ref[...], preferred_element_type=F32).astype(o_ref.dtype)


def norm_matmul(x2, gains, w, *, layer, gain, tm=512):
    T, D = x2.shape
    N = w.shape[-1]
    tm = min(tm, T)
    return pl.pallas_call(
        _norm_matmul_kernel,
        grid=(T // tm,),
        in_specs=[
            pl.BlockSpec((tm, D), lambda i: (i, 0)),
            _per_layer((1, D), gain),
            _per_layer((D, N), layer),
        ],
        out_specs=pl.BlockSpec((tm, N), lambda i: (i, 0)),
        out_shape=jax.ShapeDtypeStruct((T, N), BF16),
        compiler_params=_params(("parallel",), 40),
        name="mem_kv",
    )(x2, gains, w)


def _xattn_kernel(x_ref, kv_ref, wq_ref, wo_ref, gpre_ref, gpost_ref, o_ref, *, sub):
    dn = (((1,), (1,)), ((), ()))
    for r in range(x_ref.shape[0] // sub):
        rows = slice(r * sub, (r + 1) * sub)
        x = x_ref[rows, :]
        h = _rms(x, gpre_ref[...]).astype(BF16)
        q = jnp.dot(h, wq_ref[...], preferred_element_type=F32).astype(BF16)
        outs = []
        for hd in range(X_HEADS):
            cols = slice(hd * X_HEAD_DIM, (hd + 1) * X_HEAD_DIM)
            k = kv_ref[:, cols]
            v = kv_ref[:, D_MODEL + hd * X_HEAD_DIM:D_MODEL + (hd + 1) * X_HEAD_DIM]
            s = lax.dot_general(q[:, cols], k, dn, preferred_element_type=F32) * (X_HEAD_DIM ** -0.5)
            s = s - jnp.max(s, axis=-1, keepdims=True)
            e = jnp.exp(s)
            ev = jnp.dot(e.astype(BF16), v, preferred_element_type=F32)
            outs.append((ev / jnp.sum(e, axis=-1, keepdims=True)).astype(BF16))
        o = jnp.concatenate(outs, axis=-1)
        y = jnp.dot(o, wo_ref[...], preferred_element_type=F32)
        o_ref[rows, :] = x + _rms(y, gpost_ref[...])


def xattn(x3, kv3, w_q, w_o, gains, *, layer, gain_pre, gain_post, tm=2048, sub=512):
    B, S, D = x3.shape
    M = kv3.shape[1]
    tm = min(tm, S)
    return pl.pallas_call(
        functools.partial(_xattn_kernel, sub=min(sub, tm)),
        grid=(B, S // tm),
        in_specs=[
            pl.BlockSpec((None, tm, D), lambda b, i: (b, i, 0)),
            pl.BlockSpec((None, M, 2 * D), lambda b, i: (b, 0, 0)),
            _per_layer((D, D), layer), _per_layer((D, D), layer),
            _per_layer((1, D), gain_pre), _per_layer((1, D), gain_post),
        ],
        out_specs=pl.BlockSpec((None, tm, D), lambda b, i: (b, i, 0)),
        out_shape=jax.ShapeDtypeStruct((B, S, D), F32),
        compiler_params=_params(("parallel", "arbitrary"), 56),
        name="xattn",
    )(x3, kv3, w_q, w_o, gains, gains)


def _ffn_kernel(x_ref, w1_ref, w2_ref, gpre_ref, gpost_ref, o_ref, *, ff_chunk, sub):
    for r in range(x_ref.shape[0] // sub):
        rows = slice(r * sub, (r + 1) * sub)
        x = x_ref[rows, :]
        h = _rms(x, gpre_ref[...]).astype(BF16)
        acc = None
        for c in range(w1_ref.shape[1] // ff_chunk):
            cs = slice(c * ff_chunk, (c + 1) * ff_chunk)
            a = jnp.dot(h, w1_ref[:, cs], preferred_element_type=F32)
            a = jnp.square(jnp.maximum(a, 0.0)).astype(BF16)
            part = jnp.dot(a, w2_ref[cs, :], preferred_element_type=F32)
            acc = part if acc is None else acc + part
        o_ref[rows, :] = x + _rms(acc, gpost_ref[...])


def ffn(x2, w1, w2, gains, *, layer, gain_pre, gain_post, tm=1024, sub=512, ff_chunk=1024):
    T, D = x2.shape
    FF = w1.shape[-1]
    tm = min(tm, T)
    return pl.pallas_call(
        functools.partial(_ffn_kernel, ff_chunk=ff_chunk, sub=min(sub, tm)),
        grid=(T // tm,),
        in_specs=[
            pl.BlockSpec((tm, D), lambda i: (i, 0)),
            _per_layer((D, FF), layer),
            _per_layer((FF, D), layer),
            _per_layer((1, D), gain_pre),
            _per_layer((1, D), gain_post),
        ],
        out_specs=pl.BlockSpec((tm, D), lambda i: (i, 0)),
        out_shape=jax.ShapeDtypeStruct((T, D), F32),
        compiler_params=_params(("parallel",), 56),
        name="ffn",
    )(x2, w1, w2, gains, gains)


def _regroup_kernel(wt_ref, wide_ref, narrow_ref):
    for name in _WIDE_ORDER:
        src, size = _IN_OFFSETS[name]
        dst = _WIDE_OFFSETS[name]
        wide_ref[dst:dst + size, :] = wt_ref[src:src + size, :].astype(BF16)
    narrow_ref[...] = jnp.zeros_like(narrow_ref)
    dst = 0
    for name in _NARROW_ORDER:
        src, size = _IN_OFFSETS[name]
        narrow_ref[dst:dst + size, :] = wt_ref[src:src + size, :].astype(BF16)
        dst += size


def _regroup_in_proj(w_in, b_in, *, cols_per_step=256):
    depth, D, d_in = w_in.shape
    w_in_t = jnp.swapaxes(w_in, 1, 2)
    w_wide, w_narrow = pl.pallas_call(
        _regroup_kernel,
        grid=(depth, D // cols_per_step),
        in_specs=[pl.BlockSpec((None, d_in, cols_per_step), lambda l, i: (l, 0, i))],
        out_specs=[pl.BlockSpec((None, N_WIDE, cols_per_step), lambda l, i: (l, 0, i)),
                   pl.BlockSpec((None, LANES, cols_per_step), lambda l, i: (l, 0, i))],
        out_shape=[jax.ShapeDtypeStruct((depth, N_WIDE, D), BF16),
                   jax.ShapeDtypeStruct((depth, LANES, D), BF16)],
        compiler_params=_params(("parallel", "parallel"), 48),
        name="regroup",
    )(w_in_t)

    def cols(names):
        return [slice(_IN_OFFSETS[n][0], _IN_OFFSETS[n][0] + _IN_OFFSETS[n][1]) for n in names]
    b_wide = jnp.concatenate([b_in[..., s] for s in cols(_WIDE_ORDER)], axis=-1)[:, None, :]
    pad = LANES - N_GATE_LANES
    b_narrow = jnp.pad(jnp.concatenate([b_in[..., s] for s in cols(_NARROW_ORDER)], axis=-1),
                       ((0, 0), (0, pad)))[:, None, :]
    return w_wide, b_wide, w_narrow, b_narrow


def _prepare_params(norms, w_in, b_in, conv_w, mlstm_norm, gmlp_norm, gmlp_ws, gmlp_bs,
                    w_branch, w_out, w_xq, w_xkv, w_xo, w_ff1, w_ff2):
    depth = norms.shape[0]
    w_wide, b_wide, w_narrow, b_narrow = _regroup_in_proj(w_in, b_in)
    return dict(
        gains=norms.reshape(depth * N_NORMS, 1, D_MODEL),
        w_wide=w_wide, b_wide=b_wide, w_narrow=w_narrow, b_narrow=b_narrow,
        conv_w=conv_w, mlstm_g=mlstm_norm[:, :, None], gmlp_g=gmlp_norm[:, None, :],
        gmlp_ws=gmlp_ws, gmlp_bs_t=gmlp_bs.transpose(0, 2, 1),
        w_branch=w_branch.astype(BF16), w_out=w_out.astype(BF16),
        w_xq=w_xq.astype(BF16), w_xkv=w_xkv.astype(BF16), w_xo=w_xo.astype(BF16),
        w_ff1=w_ff1.astype(BF16), w_ff2=w_ff2.astype(BF16),
    )


def _layer(x3, mem2, p, layer, *, fox_blk, ml_chunk):
    B, S, D = x3.shape
    T = B * S
    M = mem2.shape[0] // B
    gain = lambda idx: layer * N_NORMS + idx
    x2 = x3.reshape(T, D)

    z2, zn2 = in_proj(x2, p["gains"], p["w_wide"], p["b_wide"], p["w_narrow"], p["b_narrow"],
                      layer=layer, gain=gain(NORM_MIX_PRE))
    z3 = z2.reshape(B, S, N_WIDE)
    g3, gt4 = gates(zn2.reshape(B, S, LANES), L=ml_chunk)

    y_fox = fox_attention(z3, g3, blk=fox_blk, qblk=min(2 * fox_blk, S))
    y_ml = mlstm(z3, p["conv_w"], p["mlstm_g"], g3, gt4, layer=layer, L=ml_chunk)
    y_g = gmlp(z3, p["gmlp_g"], p["gmlp_ws"], p["gmlp_bs_t"], layer=layer)
    x2 = merge(x2, z2, y_fox.reshape(T, FOX_WIDTH), y_ml.reshape(T, ML_WIDTH), y_g.reshape(T, G_WIDTH),
               p["w_branch"], p["w_out"], p["gains"], layer=layer, gain=gain(NORM_MIX_POST))

    kv = norm_matmul(mem2, p["gains"], p["w_xkv"], layer=layer, gain=gain(NORM_MEM))
    x3 = xattn(x2.reshape(B, S, D), kv.reshape(B, M, 2 * D), p["w_xq"], p["w_xo"], p["gains"],
               layer=layer, gain_pre=gain(NORM_X_PRE), gain_post=gain(NORM_X_POST))

    x2 = ffn(x3.reshape(T, D), p["w_ff1"], p["w_ff2"], p["gains"],
             layer=layer, gain_pre=gain(NORM_FF_PRE), gain_post=gain(NORM_FF_POST))
    return x2.reshape(B, S, D)


def kernel(x, mem, norms, w_in, b_in, conv_w, mlstm_norm, gmlp_norm, gmlp_ws, gmlp_bs, w_branch, w_out, w_xq, w_xkv, w_xo, w_ff1, w_ff2):
    B, M, D = mem.shape
    S = x.shape[1]
    mem2 = mem.reshape(B * M, D)
    p = _prepare_params(norms, w_in, b_in, conv_w, mlstm_norm, gmlp_norm, gmlp_ws, gmlp_bs,
                        w_branch, w_out, w_xq, w_xkv, w_xo, w_ff1, w_ff2)
    for layer in range(norms.shape[0]):
        x = _layer(x, mem2, p, layer, fox_blk=min(256, S), ml_chunk=min(ML_CHUNK, S))
    return x
```
